```python
import jax, jax.numpy as jnp
from jax import lax
import numpy as np

D_MODEL = 1024
BATCH = 4
SEQ = 8192
DEPTH = 2

MLA_HEADS = 8
QK_NOPE_DIM = 64
QK_ROPE_DIM = 32
V_HEAD_DIM = 64
Q_LORA_RANK = 384
KV_LORA_RANK = 256
ROPE_THETA = 10000.0
Q_BLOCK = 128
MLA_WIDTH = MLA_HEADS * V_HEAD_DIM
POOL_WINDOWS = (2, 4, 8, 16)
POOL_GROUP_DIM = 64
POOL_WIDTH = len(POOL_WINDOWS) * POOL_GROUP_DIM
SG_HEADS = 4
SG_HEAD_DIM = 64
SG_WIDTH = SG_HEADS * SG_HEAD_DIM
SG_CHUNK = 128
D_MIX = MLA_WIDTH + POOL_WIDTH + SG_WIDTH
IN_COLS = Q_LORA_RANK + KV_LORA_RANK + QK_ROPE_DIM + POOL_WIDTH + 2 * SG_WIDTH
D_FF = -(-(8 * D_MODEL) // (3 * 256)) * 256
EPS = 1e-6

kernel_name = "hybrid_mla_pool_sgmlp_encoder"


def rmsnorm(x, g):
    xf = x.astype(jnp.float32)
    y = xf * lax.rsqrt(jnp.mean(xf * xf, axis=-1, keepdims=True) + EPS)
    return (y * g.astype(jnp.float32)).astype(x.dtype)


def rope_tables(positions):
    inv_freq = ROPE_THETA ** (-jnp.arange(0, QK_ROPE_DIM, 2, dtype=jnp.float32) / QK_ROPE_DIM)
    ang = positions.astype(jnp.float32)[..., None] * inv_freq
    return jnp.cos(ang), jnp.sin(ang)


def apply_rope(t, cos, sin):
    tf = t.astype(jnp.float32)
    half = QK_ROPE_DIM // 2
    t1, t2 = tf[..., :half], tf[..., half:]
    out = jnp.concatenate([t1 * cos - t2 * sin, t1 * sin + t2 * cos], axis=-1)
    return out.astype(t.dtype)


def mla_mixer(h_q, h_kv, k_rope_in, positions, q_norm_g, kv_norm_g, w_uq, w_ukv):
    B, S, _ = h_q.shape
    cq = rmsnorm(h_q, q_norm_g)
    q = jnp.einsum('bsr,rn->bsn', cq, w_uq).reshape(B, S, MLA_HEADS, QK_NOPE_DIM + QK_ROPE_DIM)
    q_nope, q_rope = q[..., :QK_NOPE_DIM], q[..., QK_NOPE_DIM:]
    ckv = rmsnorm(h_kv, kv_norm_g)
    kv = jnp.einsum('bsr,rn->bsn', ckv, w_ukv).reshape(B, S, MLA_HEADS, QK_NOPE_DIM + V_HEAD_DIM)
    k_nope, v = kv[..., :QK_NOPE_DIM], kv[..., QK_NOPE_DIM:]
    cos, sin = rope_tables(positions)
    q_rope = apply_rope(q_rope, cos[:, :, None, :], sin[:, :, None, :])
    k_rope = apply_rope(k_rope_in, cos, sin)
    scale = (QK_NOPE_DIM + QK_ROPE_DIM) ** -0.5
    nb = S // Q_BLOCK

    def blockify(t):
        return jnp.moveaxis(t.reshape(B, nb, Q_BLOCK, *t.shape[2:]), 1, 0)

    def attend(blk):
        qn, qr = blk
        s = (jnp.einsum('bqhd,bkhd->bhqk', qn, k_nope)
             + jnp.einsum('bqhr,bkr->bhqk', qr, k_rope))
        p = jax.nn.softmax(s.astype(jnp.float32) * scale, axis=-1).astype(v.dtype)
        return jnp.einsum('bhqk,bkhd->bqhd', p, v)

    o = lax.map(attend, (blockify(q_nope), blockify(q_rope)))
    return jnp.moveaxis(o, 0, 1).reshape(B, S, MLA_WIDTH)


def pool_mixer(h, w_pool, pool_scale):
    B, S, _ = h.shape
    hf = h.astype(jnp.float32)
    cs = jnp.concatenate([jnp.zeros((B, 1, POOL_WIDTH), jnp.float32), lax.cumsum(hf, axis=1)], axis=1)
    t = jnp.arange(S)
    outs = []
    for g, w in enumerate(POOL_WINDOWS):
        left = w // 2
        right = w - 1 - left
        lo = jnp.clip(t - left, 0, S)
        hi = jnp.clip(t + right + 1, 0, S)
        sl = slice(g * POOL_GROUP_DIM, (g + 1) * POOL_GROUP_DIM)
        csg = cs[:, :, sl]
        mean = (csg[:, hi] - csg[:, lo]) / (hi - lo).astype(jnp.float32)[None, :, None]
        d = (mean - hf[:, :, sl]).astype(h.dtype)
        outs.append(jnp.einsum('bsc,cd->bsd', d, w_pool[g]))
    return jnp.concatenate(outs, axis=-1) * pool_scale


def sg_mixer(h_uv, sg_norm_g, w_s, b_s):
    B, S, _ = h_uv.shape
    z = jax.nn.gelu(h_uv)
    u, v = z[..., :SG_WIDTH], z[..., SG_WIDTH:]
    v = rmsnorm(v.reshape(B, S, SG_HEADS, SG_HEAD_DIM), sg_norm_g.reshape(SG_HEADS, SG_HEAD_DIM))
    vc = v.reshape(B, S // SG_CHUNK, SG_CHUNK, SG_HEADS, SG_HEAD_DIM)
    mixed = jnp.einsum('gpq,bcqgd->bcpgd', w_s, vc) + b_s.T[None, None, :, :, None]
    return u * mixed.reshape(B, S, SG_WIDTH)


def setup_inputs(seed: int = 0) -> dict:
    key = jax.random.key(seed)
    ks = jax.random.split(key, 20)
    f32 = jnp.float32

    def nrm(k, shape, fan_in):
        return jax.random.normal(k, shape, f32) * (fan_in ** -0.5)

    def gain(k, shape):
        return 1.0 + 0.05 * jax.random.normal(k, shape, f32)

    x = jax.random.normal(ks[0], (BATCH, SEQ, D_MODEL), f32)
    positions = jnp.broadcast_to(jnp.arange(SEQ, dtype=jnp.int32), (BATCH, SEQ))
    return {
        "x": x,
        "positions": positions,
        "mix_norm": gain(ks[1], (DEPTH, D_MODEL)),
        "w_in": nrm(ks[2], (DEPTH, D_MODEL, IN_COLS), D_MODEL),
        "q_norm": gain(ks[3], (DEPTH, Q_LORA_RANK)),
        "kv_norm": gain(ks[4], (DEPTH, KV_LORA_RANK)),
        "w_uq": nrm(ks[5], (DEPTH, Q_LORA_RANK, MLA_HEADS * (QK_NOPE_DIM + QK_ROPE_DIM)), Q_LORA_RANK),
        "w_ukv": nrm(ks[6], (DEPTH, KV_LORA_RANK, MLA_HEADS * (QK_NOPE_DIM + V_HEAD_DIM)), KV_LORA_RANK),
        "w_pool": nrm(ks[7], (DEPTH, len(POOL_WINDOWS), POOL_GROUP_DIM, POOL_GROUP_DIM), POOL_GROUP_DIM),
        "pool_scale": gain(ks[8], (DEPTH, POOL_WIDTH)),
        "sg_norm": gain(ks[9], (DEPTH, SG_WIDTH)),
        "w_s": nrm(ks[10], (DEPTH, SG_HEADS, SG_CHUNK, SG_CHUNK), SG_CHUNK),
        "b_s": 1.0 + 0.05 * jax.random.normal(ks[11], (DEPTH, SG_HEADS, SG_CHUNK), f32),
        "w_o": nrm(ks[12], (DEPTH, D_MIX, D_MODEL), D_MIX),
        "ffn_norm": gain(ks[13], (DEPTH, D_MODEL)),
        "w_gate": nrm(ks[14], (DEPTH, D_MODEL, D_FF), D_MODEL),
        "w_up": nrm(ks[15], (DEPTH, D_MODEL, D_FF), D_MODEL),
        "w_down": nrm(ks[16], (DEPTH, D_FF, D_MODEL), D_FF),
        "final_norm": gain(ks[17], (D_MODEL,)),
    }


def reference(x, positions, mix_norm, w_in, q_norm, kv_norm, w_uq, w_ukv, w_pool, pool_scale,
              sg_norm, w_s, b_s, w_o, ffn_norm, w_gate, w_up, w_down, final_norm):
    o1 = Q_LORA_RANK
    o2 = o1 + KV_LORA_RANK
    o3 = o2 + QK_ROPE_DIM
    o4 = o3 + POOL_WIDTH
    for l in range(DEPTH):
        h = rmsnorm(x, mix_norm[l])
        p = jnp.einsum('bsd,dn->bsn', h, w_in[l])
        a = mla_mixer(p[..., :o1], p[..., o1:o2], p[..., o2:o3], positions,
                      q_norm[l], kv_norm[l], w_uq[l], w_ukv[l])
        b = pool_mixer(p[..., o3:o4], w_pool[l], pool_scale[l])
        c = sg_mixer(p[..., o4:], sg_norm[l], w_s[l], b_s[l])
        mix = jnp.concatenate([a, b, c], axis=-1)
        x = x + jnp.einsum('bsn,nd->bsd', mix, w_o[l])
        h = rmsnorm(x, ffn_norm[l])
        g = jnp.einsum('bsd,df->bsf', h, w_gate[l])
        u = jnp.einsum('bsd,df->bsf', h, w_up[l])
        x = x + jnp.einsum('bsf,fd->bsd', jax.nn.silu(g) * u, w_down[l])
    return rmsnorm(x, final_norm)
```

```python
import functools

import jax
import jax.numpy as jnp
from jax import lax
from jax.experimental import pallas as pl
from jax.experimental.pallas import tpu as pltpu

D_MODEL = 1024
MLA_HEADS = 8
QK_NOPE_DIM = 64
QK_ROPE_DIM = 32
V_HEAD_DIM = 64
Q_LORA_RANK = 384
KV_LORA_RANK = 256
ROPE_THETA = 10000.0
MLA_WIDTH = MLA_HEADS * V_HEAD_DIM
POOL_WINDOWS = (2, 4, 8, 16)
POOL_GROUP_DIM = 64
POOL_WIDTH = len(POOL_WINDOWS) * POOL_GROUP_DIM
SG_HEADS = 4
SG_HEAD_DIM = 64
SG_WIDTH = SG_HEADS * SG_HEAD_DIM
SG_CHUNK = 128
D_FF = 2816
EPS = 1e-6

LANES = 128
HEAD_PAD = 128
ROPE_OFF = QK_NOPE_DIM
POOL_HALO = 8
FF_CHUNK = 256

C_Q = 0
C_KV = C_Q + Q_LORA_RANK
C_POOL = C_KV + KV_LORA_RANK
C_SG = C_POOL + POOL_WIDTH
C_ROPE = C_SG + 2 * SG_WIDTH
C_ROT = C_ROPE + LANES
IN_COLS_PAD = C_ROT + LANES

TM_PROJ = 512
TM_POST = 512
TQ = 512
TK = 512
TM_ROPE = 2048

VMEM_LIMIT = 56 * 1024 * 1024

f32 = jnp.float32
bf16 = jnp.bfloat16


def _rms(x, g):
    return x * lax.rsqrt(jnp.mean(x * x, axis=-1, keepdims=True) + EPS) * g


def _dot(a, b):
    return jnp.dot(a, b, preferred_element_type=f32)


def _dot_nt(a, b):
    return lax.dot_general(a, b, (((1,), (1,)), ((), ())), preferred_element_type=f32)


def _rope_kernel(pos_row_ref, pos_col_ref, inv_col_ref, inv_row_ref, mask_row_ref,
                 cos_t_ref, sin_t_ref, cos_k_ref, sin_k_ref):
    ang_t = pos_row_ref[...].astype(f32) * inv_col_ref[...]
    cos_t_ref[...] = jnp.cos(ang_t)
    sin_t_ref[...] = jnp.sin(ang_t)
    ang_k = pos_col_ref[...].astype(f32) * inv_row_ref[...]
    cos_k_ref[...] = jnp.cos(ang_k) * mask_row_ref[...]
    sin_k_ref[...] = jnp.sin(ang_k) * mask_row_ref[...]


def _rope_tables(positions):
    n = positions.size
    inv_freq = ROPE_THETA ** (-jnp.arange(0, QK_ROPE_DIM, 2, dtype=f32) / QK_ROPE_DIM)
    inv2 = jnp.concatenate([inv_freq, inv_freq])
    zeros = jnp.zeros
    inv_row = jnp.concatenate([zeros((ROPE_OFF,), f32), inv2, zeros((HEAD_PAD - ROPE_OFF - QK_ROPE_DIM,), f32)])
    mask_row = jnp.concatenate([zeros((ROPE_OFF,), f32), jnp.ones((QK_ROPE_DIM,), f32),
                                zeros((HEAD_PAD - ROPE_OFF - QK_ROPE_DIM,), f32)])
    tm = TM_ROPE
    return pl.pallas_call(
        _rope_kernel,
        grid=(n // tm,),
        in_specs=[
            pl.BlockSpec((1, tm), lambda i: (0, i)),
            pl.BlockSpec((tm, 1), lambda i: (i, 0)),
            pl.BlockSpec((QK_ROPE_DIM, 1), lambda i: (0, 0)),
            pl.BlockSpec((1, HEAD_PAD), lambda i: (0, 0)),
            pl.BlockSpec((1, HEAD_PAD), lambda i: (0, 0)),
        ],
        out_specs=[
            pl.BlockSpec((QK_ROPE_DIM, tm), lambda i: (0, i)),
            pl.BlockSpec((QK_ROPE_DIM, tm), lambda i: (0, i)),
            pl.BlockSpec((tm, HEAD_PAD), lambda i: (i, 0)),
            pl.BlockSpec((tm, HEAD_PAD), lambda i: (i, 0)),
        ],
        out_shape=[
            jax.ShapeDtypeStruct((QK_ROPE_DIM, n), f32),
            jax.ShapeDtypeStruct((QK_ROPE_DIM, n), f32),
            jax.ShapeDtypeStruct((n, HEAD_PAD), f32),
            jax.ShapeDtypeStruct((n, HEAD_PAD), f32),
        ],
        compiler_params=pltpu.CompilerParams(dimension_semantics=("parallel",)),
        name="rope_tables",
    )(positions.reshape(1, n), positions.reshape(n, 1), inv2.reshape(QK_ROPE_DIM, 1),
      inv_row.reshape(1, HEAD_PAD), mask_row.reshape(1, HEAD_PAD))


def _proj_kernel(x_ref, g_ref, win_ref, qn_ref, kvn_ref, wuq_t_ref, wk_ref, wuv_t_ref,
                 ws_ref, bias_ref, sgn_ref, cos_t_ref, sin_t_ref, cos_k_ref, sin_k_ref,
                 q_t_ref, k_ref, v_t_ref, hp_ref, c_ref):
    tm = x_ref.shape[0]
    scale = (QK_NOPE_DIM + QK_ROPE_DIM) ** -0.5
    h = _rms(x_ref[...], g_ref[...]).astype(bf16)
    p = _dot(h, win_ref[...])

    cq = _rms(p[:, C_Q:C_Q + Q_LORA_RANK], qn_ref[...]).astype(bf16)
    q_t = _dot_nt(wuq_t_ref[...], cq)
    cos_t = cos_t_ref[...]
    sin_t = sin_t_ref[...]
    for hd in range(MLA_HEADS):
        b0 = hd * HEAD_PAD
        nope = q_t[b0:b0 + QK_NOPE_DIM]
        rope = q_t[b0 + ROPE_OFF:b0 + ROPE_OFF + QK_ROPE_DIM]
        rot = q_t[b0 + ROPE_OFF + QK_ROPE_DIM:b0 + HEAD_PAD]
        q_t_ref[b0:b0 + QK_NOPE_DIM, :] = (nope * scale).astype(bf16)
        q_t_ref[b0 + ROPE_OFF:b0 + ROPE_OFF + QK_ROPE_DIM, :] = (
            (rope * cos_t + rot * sin_t) * scale).astype(bf16)
        q_t_ref[b0 + ROPE_OFF + QK_ROPE_DIM:b0 + HEAD_PAD, :] = jnp.zeros(
            (HEAD_PAD - ROPE_OFF - QK_ROPE_DIM, tm), bf16)

    ckv = _rms(p[:, C_KV:C_KV + KV_LORA_RANK], kvn_ref[...]).astype(bf16)
    k_nope = _dot(ckv, wk_ref[...])
    k_rope = (p[:, C_ROPE:C_ROPE + LANES] * cos_k_ref[...]
              + p[:, C_ROT:C_ROT + LANES] * sin_k_ref[...])
    for hd in range(MLA_HEADS):
        b0 = hd * HEAD_PAD
        k_ref[:, b0:b0 + HEAD_PAD] = (k_nope[:, b0:b0 + HEAD_PAD] + k_rope).astype(bf16)
    v_t_ref[...] = _dot_nt(wuv_t_ref[...], ckv).astype(bf16)

    hp_ref[...] = p[:, C_POOL:C_POOL + POOL_WIDTH]

    z = jax.nn.gelu(p[:, C_SG:C_SG + 2 * SG_WIDTH])
    u = z[:, :SG_WIDTH]
    v = z[:, SG_WIDTH:]
    grp = lax.broadcasted_iota(jnp.int32, (1, SG_WIDTH), 1) // SG_HEAD_DIM
    vsq = v * v
    inv = jnp.zeros_like(v)
    for g in range(SG_HEADS):
        sel = grp == g
        ms = jnp.sum(jnp.where(sel, vsq, 0.0), axis=-1, keepdims=True) / SG_HEAD_DIM
        inv = jnp.where(sel, lax.rsqrt(ms + EPS), inv)
    vn = (v * inv * sgn_ref[...]).astype(bf16)
    bias = bias_ref[...]
    for c in range(tm // SG_CHUNK):
        r0 = c * SG_CHUNK
        vc = vn[r0:r0 + SG_CHUNK]
        mixed = bias
        for g in range(SG_HEADS):
            mixed = mixed + jnp.where(grp == g, _dot(ws_ref[g], vc), 0.0)
        c_ref[r0:r0 + SG_CHUNK, :] = (u[r0:r0 + SG_CHUNK] * mixed).astype(bf16)


def _const_spec(shape):
    nd = len(shape)
    return pl.BlockSpec(shape, lambda *_: (0,) * nd)


def _proj_call(x2, w, tables, batch, seq):
    n = x2.shape[0]
    tm = TM_PROJ
    cos_t, sin_t, cos_k, sin_k = tables
    row = lambda cols: pl.BlockSpec((tm, cols), lambda i: (i, 0))
    col = lambda rows: pl.BlockSpec((rows, tm), lambda i: (0, i))
    tiles_per_seq = seq // tm
    q_spec = pl.BlockSpec((None, MLA_HEADS * HEAD_PAD, tm), lambda i: (i // tiles_per_seq, 0, i % tiles_per_seq))
    v_spec = pl.BlockSpec((None, MLA_WIDTH, tm), lambda i: (i // tiles_per_seq, 0, i % tiles_per_seq))
    k_spec = pl.BlockSpec((None, tm, MLA_HEADS * HEAD_PAD), lambda i: (i // tiles_per_seq, i % tiles_per_seq, 0))
    return pl.pallas_call(
        _proj_kernel,
        grid=(n // tm,),
        in_specs=[
            row(D_MODEL),
            _const_spec((1, D_MODEL)),
            _const_spec((D_MODEL, IN_COLS_PAD)),
            _const_spec((1, Q_LORA_RANK)),
            _const_spec((1, KV_LORA_RANK)),
            _const_spec((MLA_HEADS * HEAD_PAD, Q_LORA_RANK)),
            _const_spec((KV_LORA_RANK, MLA_HEADS * HEAD_PAD)),
            _const_spec((MLA_WIDTH, KV_LORA_RANK)),
            _const_spec((SG_HEADS, SG_CHUNK, SG_CHUNK)),
            _const_spec((SG_CHUNK, SG_WIDTH)),
            _const_spec((1, SG_WIDTH)),
            col(QK_ROPE_DIM),
            col(QK_ROPE_DIM),
            row(HEAD_PAD),
            row(HEAD_PAD),
        ],
        out_specs=[q_spec, k_spec, v_spec, row(POOL_WIDTH), row(SG_WIDTH)],
        out_shape=[
            jax.ShapeDtypeStruct((batch, MLA_HEADS * HEAD_PAD, seq), bf16),
            jax.ShapeDtypeStruct((batch, seq, MLA_HEADS * HEAD_PAD), bf16),
            jax.ShapeDtypeStruct((batch, MLA_WIDTH, seq), bf16),
            jax.ShapeDtypeStruct((n, POOL_WIDTH), f32),
            jax.ShapeDtypeStruct((n, SG_WIDTH), bf16),
        ],
        compiler_params=pltpu.CompilerParams(dimension_semantics=("parallel",), vmem_limit_bytes=VMEM_LIMIT),
        name="proj",
    )(x2, w["mix_norm"], w["w_in"], w["q_norm"], w["kv_norm"], w["w_uq_t"], w["w_k"], w["w_uv_t"],
      w["w_s"], w["sg_bias"], w["sg_norm"], cos_t, sin_t, cos_k, sin_k)


def _attn_kernel(q_t_ref, k_ref, v_t_ref, o_ref, *, tk):
    seq = k_ref.shape[0]
    n_kv = seq // tk
    outs = []
    for hh in range(2):
        q_t = q_t_ref[hh * HEAD_PAD:(hh + 1) * HEAD_PAD, :]

        def chunk(i):
            off = pl.multiple_of(i * tk, tk)
            kc = k_ref[pl.ds(off, tk), hh * HEAD_PAD:(hh + 1) * HEAD_PAD]
            vc = v_t_ref[hh * V_HEAD_DIM:(hh + 1) * V_HEAD_DIM, pl.ds(off, tk)]
            return _dot(kc, q_t), vc

        s_t, vc = chunk(0)
        m = jnp.max(s_t, axis=0, keepdims=True)
        p_t = jnp.exp(s_t - m)
        l = jnp.sum(p_t, axis=0, keepdims=True)
        acc = _dot(vc, p_t.astype(bf16))

        def body(i, carry):
            m, l, acc = carry
            s_t, vc = chunk(i)
            m_new = jnp.maximum(m, jnp.max(s_t, axis=0, keepdims=True))
            alpha = jnp.exp(m - m_new)
            p_t = jnp.exp(s_t - m_new)
            l = alpha * l + jnp.sum(p_t, axis=0, keepdims=True)
            acc = alpha * acc + _dot(vc, p_t.astype(bf16))
            return m_new, l, acc

        m, l, acc = lax.fori_loop(1, n_kv, body, (m, l, acc))
        outs.append(acc / l)
    o_t = jnp.concatenate(outs, axis=0)
    o_ref[...] = o_t.T.astype(o_ref.dtype)


def _attn_call(q_t, k, v_t):
    batch, _, seq = q_t.shape
    pairs = MLA_HEADS // 2
    return pl.pallas_call(
        functools.partial(_attn_kernel, tk=TK),
        grid=(batch, pairs, seq // TQ),
        in_specs=[
            pl.BlockSpec((None, 2 * HEAD_PAD, TQ), lambda b, h, q: (b, h, q)),
            pl.BlockSpec((None, seq, 2 * HEAD_PAD), lambda b, h, q: (b, 0, h)),
            pl.BlockSpec((None, 2 * V_HEAD_DIM, seq), lambda b, h, q: (b, h, 0)),
        ],
        out_specs=pl.BlockSpec((None, TQ, 2 * V_HEAD_DIM), lambda b, h, q: (b, q, h)),
        out_shape=jax.ShapeDtypeStruct((batch, seq, MLA_WIDTH), bf16),
        compiler_params=pltpu.CompilerParams(
            dimension_semantics=("parallel", "parallel", "arbitrary"), vmem_limit_bytes=VMEM_LIMIT),
        name="attn",
    )(q_t, k, v_t)


def _post_kernel(x_ref, a_ref, hp_ref, hp_prev_ref, hp_next_ref, c_ref, wo_ref, wpool_ref, pscale_ref,
                 fn_ref, wg_ref, wu_ref, wd_ref, final_ref, o_ref, ext_ref, *, seq, last):
    tm = x_ref.shape[0]
    tiles_per_seq = seq // tm
    si = pl.program_id(0) % tiles_per_seq
    hp = hp_ref[...]

    ext_ref[0:POOL_HALO, :] = jnp.where(si > 0, hp_prev_ref[...], 0.0)
    ext_ref[POOL_HALO:POOL_HALO + tm, :] = hp
    ext_ref[POOL_HALO + tm:2 * POOL_HALO + tm, :] = jnp.where(si < tiles_per_seq - 1, hp_next_ref[...], 0.0)

    def shifted(j):
        return ext_ref[POOL_HALO + j:POOL_HALO + j + tm, :]

    grp = lax.broadcasted_iota(jnp.int32, (1, POOL_WIDTH), 1) // POOL_GROUP_DIM
    t_abs = si * tm + lax.broadcasted_iota(jnp.int32, (tm, 1), 0)
    win_sum = None
    summed = jnp.zeros((tm, POOL_WIDTH), f32)
    cnt = jnp.zeros((tm, POOL_WIDTH), f32)
    lo_prev, hi_prev = 0, 0
    acc = None
    for g, w in enumerate(POOL_WINDOWS):
        left = w // 2
        right = w - 1 - left
        for j in list(range(-left, lo_prev)) + list(range(hi_prev, right + 1)):
            acc = shifted(j) if acc is None else acc + shifted(j)
        lo_prev, hi_prev = -left, right + 1
        n_in = (jnp.minimum(t_abs + right + 1, seq) - jnp.maximum(t_abs - left, 0)).astype(f32)
        summed = jnp.where(grp == g, acc, summed)
        cnt = jnp.where(grp == g, n_in, cnt)
    d = (summed / cnt - hp).astype(bf16)
    b = _dot(d, wpool_ref[...]) * pscale_ref[...]

    mixo = (_dot(a_ref[...], wo_ref[0:MLA_WIDTH, :])
            + _dot(b.astype(bf16), wo_ref[MLA_WIDTH:MLA_WIDTH + POOL_WIDTH, :])
            + _dot(c_ref[...], wo_ref[MLA_WIDTH + POOL_WIDTH:, :]))
    x1 = x_ref[...] + mixo

    hn = _rms(x1, fn_ref[...]).astype(bf16)
    y = None
    for f in range(D_FF // FF_CHUNK):
        f0 = f * FF_CHUNK
        gte = _dot(hn, wg_ref[:, f0:f0 + FF_CHUNK])
        up = _dot(hn, wu_ref[:, f0:f0 + FF_CHUNK])
        act = (jax.nn.silu(gte) * up).astype(bf16)
        part = _dot(act, wd_ref[f0:f0 + FF_CHUNK, :])
        y = part if y is None else y + part
    out = x1 + y
    if last:
        out = _rms(out, final_ref[...])
    o_ref[...] = out


def _post_call(x2, a2, hp, c, w, final_norm, seq, last):
    n = x2.shape[0]
    tm = TM_POST
    hb = tm // POOL_HALO
    n_halo_blocks = n // POOL_HALO
    row = lambda cols: pl.BlockSpec((tm, cols), lambda i: (i, 0))
    resident = lambda shape: pl.BlockSpec(shape, lambda i: (0,) * len(shape), pipeline_mode=pl.Buffered(1))
    return pl.pallas_call(
        functools.partial(_post_kernel, seq=seq, last=last),
        grid=(n // tm,),
        in_specs=[
            row(D_MODEL),
            row(MLA_WIDTH),
            row(POOL_WIDTH),
            pl.BlockSpec((POOL_HALO, POOL_WIDTH), lambda i: (jnp.maximum(i * hb - 1, 0), 0)),
            pl.BlockSpec((POOL_HALO, POOL_WIDTH), lambda i: (jnp.minimum((i + 1) * hb, n_halo_blocks - 1), 0)),
            row(SG_WIDTH),
            resident((D_MODEL, D_MODEL)),
            resident((POOL_WIDTH, POOL_WIDTH)),
            _const_spec((1, POOL_WIDTH)),
            _const_spec((1, D_MODEL)),
            resident((D_MODEL, D_FF)),
            resident((D_MODEL, D_FF)),
            resident((D_FF, D_MODEL)),
            _const_spec((1, D_MODEL)),
        ],
        out_specs=row(D_MODEL),
        out_shape=jax.ShapeDtypeStruct((n, D_MODEL), f32),
        scratch_shapes=[pltpu.VMEM((tm + 2 * POOL_HALO, POOL_WIDTH), f32)],
        compiler_params=pltpu.CompilerParams(dimension_semantics=("parallel",), vmem_limit_bytes=VMEM_LIMIT),
        name="post_last" if last else "post",
    )(x2, a2, hp, hp, hp, c, w["w_o"], w["w_pool"], w["pool_scale"], w["ffn_norm"],
      w["w_gate"], w["w_up"], w["w_down"], final_norm)


def _prep_layer(l, mix_norm, w_in, q_norm, kv_norm, w_uq, w_ukv, w_pool, pool_scale, sg_norm, w_s, b_s,
                w_o, ffn_norm, w_gate, w_up, w_down):
    o1 = Q_LORA_RANK
    o2 = o1 + KV_LORA_RANK
    o3 = o2 + QK_ROPE_DIM
    o4 = o3 + POOL_WIDTH
    half = QK_ROPE_DIM // 2

    def rot_cols(t):
        return jnp.concatenate([-t[..., half:], t[..., :half]], axis=-1)

    def pad_rope(t):
        z = jnp.zeros
        return jnp.concatenate([z((D_MODEL, ROPE_OFF), f32), t, z((D_MODEL, LANES - ROPE_OFF - QK_ROPE_DIM), f32)], axis=1)

    wi = w_in[l]
    w_kr = wi[:, o2:o3]
    w_in_pad = jnp.concatenate([wi[:, :o2], wi[:, o3:], pad_rope(w_kr), pad_rope(rot_cols(w_kr))], axis=1)

    wq = w_uq[l].reshape(Q_LORA_RANK, MLA_HEADS, QK_NOPE_DIM + QK_ROPE_DIM)
    wq_rope = wq[..., QK_NOPE_DIM:]
    wq_pad = jnp.concatenate([wq[..., :QK_NOPE_DIM], wq_rope, rot_cols(wq_rope)], axis=-1)
    w_uq_t = wq_pad.reshape(Q_LORA_RANK, MLA_HEADS * HEAD_PAD).T

    wkv = w_ukv[l].reshape(KV_LORA_RANK, MLA_HEADS, QK_NOPE_DIM + V_HEAD_DIM)
    w_k = jnp.concatenate([wkv[..., :QK_NOPE_DIM],
                           jnp.zeros((KV_LORA_RANK, MLA_HEADS, HEAD_PAD - QK_NOPE_DIM), f32)], axis=-1)
    w_k = w_k.reshape(KV_LORA_RANK, MLA_HEADS * HEAD_PAD)
    w_uv_t = wkv[..., QK_NOPE_DIM:].reshape(KV_LORA_RANK, MLA_WIDTH).T

    n_groups = len(POOL_WINDOWS)
    eye = jnp.eye(n_groups, dtype=f32)
    w_pool_bd = (eye[:, None, :, None] * w_pool[l][:, :, None, :]).reshape(POOL_WIDTH, POOL_WIDTH)

    sg_bias = jnp.repeat(b_s[l].T, SG_HEAD_DIM, axis=1)
    return dict(
        mix_norm=mix_norm[l].reshape(1, D_MODEL),
        w_in=w_in_pad.astype(bf16),
        q_norm=q_norm[l].reshape(1, Q_LORA_RANK),
        kv_norm=kv_norm[l].reshape(1, KV_LORA_RANK),
        w_uq_t=w_uq_t.astype(bf16),
        w_k=w_k.astype(bf16),
        w_uv_t=w_uv_t.astype(bf16),
        w_s=w_s[l].astype(bf16),
        sg_bias=sg_bias,
        sg_norm=sg_norm[l].reshape(1, SG_WIDTH),
        w_pool=w_pool_bd.astype(bf16),
        pool_scale=pool_scale[l].reshape(1, POOL_WIDTH),
        w_o=w_o[l].astype(bf16),
        ffn_norm=ffn_norm[l].reshape(1, D_MODEL),
        w_gate=w_gate[l].astype(bf16),
        w_up=w_up[l].astype(bf16),
        w_down=w_down[l].astype(bf16),
    )


def kernel(x, positions, mix_norm, w_in, q_norm, kv_norm, w_uq, w_ukv, w_pool, pool_scale, sg_norm, w_s, b_s,
           w_o, ffn_norm, w_gate, w_up, w_down, final_norm):
    batch, seq, _ = x.shape
    depth = w_in.shape[0]
    assert seq % TM_PROJ == 0 and seq % TM_POST == 0 and seq % TQ == 0 and seq % TK == 0
    tables = _rope_tables(positions)
    x2 = x.reshape(batch * seq, D_MODEL)
    fin = final_norm.reshape(1, D_MODEL)
    for l in range(depth):
        w = _prep_layer(l, mix_norm, w_in, q_norm, kv_norm, w_uq, w_ukv, w_pool, pool_scale, sg_norm, w_s, b_s,
                        w_o, ffn_norm, w_gate, w_up, w_down)
        q_t, k, v_t, hp, c = _proj_call(x2, w, tables, batch, seq)
        a = _attn_call(q_t, k, v_t)
        x2 = _post_call(x2, a.reshape(batch * seq, MLA_WIDTH), hp, c, w, fin, seq, last=(l == depth - 1))
    return x2.reshape(batch, seq, D_MODEL)
```

```python
import functools

import jax
import jax.numpy as jnp
from jax import lax
from jax.experimental import pallas as pl
from jax.experimental.pallas import tpu as pltpu

D_MODEL = 1024
MLA_HEADS = 8
QK_NOPE_DIM = 64
QK_ROPE_DIM = 32
V_HEAD_DIM = 64
Q_LORA_RANK = 384
KV_LORA_RANK = 256
ROPE_THETA = 10000.0
MLA_WIDTH = MLA_HEADS * V_HEAD_DIM
POOL_WINDOWS = (2, 4, 8, 16)
POOL_GROUP_DIM = 64
POOL_WIDTH = len(POOL_WINDOWS) * POOL_GROUP_DIM
SG_HEADS = 4
SG_HEAD_DIM = 64
SG_WIDTH = SG_HEADS * SG_HEAD_DIM
SG_CHUNK = 128
D_FF = 2816
EPS = 1e-6
LOG2_E = 1.4426950408889634

LANES = 128
HEAD_PAD = 128
ROPE_OFF = QK_NOPE_DIM
POOL_HALO = 8
FF_CHUNK = 256

C_Q = 0
C_KV = C_Q + Q_LORA_RANK
C_POOL = C_KV + KV_LORA_RANK
C_SG = C_POOL + POOL_WIDTH
C_ROPE = C_SG + 2 * SG_WIDTH
C_ROT = C_ROPE + LANES
IN_COLS_PAD = C_ROT + LANES

TM_PROJ = 512
TM_POST = 512
TQ = 512
TK = 512
TM_ROPE = 2048

VMEM_LIMIT = 56 * 1024 * 1024

f32 = jnp.float32
bf16 = jnp.bfloat16


def _rms(x, g):
    return x * lax.rsqrt(jnp.mean(x * x, axis=-1, keepdims=True) + EPS) * g


def _dot(a, b):
    return jnp.dot(a, b, preferred_element_type=f32)


def _dot_nt(a, b):
    return lax.dot_general(a, b, (((1,), (1,)), ((), ())), preferred_element_type=f32)


def _rope_kernel(pos_row_ref, pos_col_ref, inv_col_ref, inv_row_ref, mask_row_ref,
                 cos_t_ref, sin_t_ref, cos_k_ref, sin_k_ref):
    ang_t = pos_row_ref[...].astype(f32) * inv_col_ref[...]
    cos_t_ref[...] = jnp.cos(ang_t)
    sin_t_ref[...] = jnp.sin(ang_t)
    ang_k = pos_col_ref[...].astype(f32) * inv_row_ref[...]
    cos_k_ref[...] = jnp.cos(ang_k) * mask_row_ref[...]
    sin_k_ref[...] = jnp.sin(ang_k) * mask_row_ref[...]


def _rope_tables(positions):
    n = positions.size
    inv_freq = ROPE_THETA ** (-jnp.arange(0, QK_ROPE_DIM, 2, dtype=f32) / QK_ROPE_DIM)
    inv2 = jnp.concatenate([inv_freq, inv_freq])
    zeros = jnp.zeros
    inv_row = jnp.concatenate([zeros((ROPE_OFF,), f32), inv2, zeros((HEAD_PAD - ROPE_OFF - QK_ROPE_DIM,), f32)])
    mask_row = jnp.concatenate([zeros((ROPE_OFF,), f32), jnp.ones((QK_ROPE_DIM,), f32),
                                zeros((HEAD_PAD - ROPE_OFF - QK_ROPE_DIM,), f32)])
    tm = TM_ROPE
    return pl.pallas_call(
        _rope_kernel,
        grid=(n // tm,),
        in_specs=[
            pl.BlockSpec((1, tm), lambda i: (0, i)),
            pl.BlockSpec((tm, 1), lambda i: (i, 0)),
            pl.BlockSpec((QK_ROPE_DIM, 1), lambda i: (0, 0)),
            pl.BlockSpec((1, HEAD_PAD), lambda i: (0, 0)),
            pl.BlockSpec((1, HEAD_PAD), lambda i: (0, 0)),
        ],
        out_specs=[
            pl.BlockSpec((QK_ROPE_DIM, tm), lambda i: (0, i)),
            pl.BlockSpec((QK_ROPE_DIM, tm), lambda i: (0, i)),
            pl.BlockSpec((tm, HEAD_PAD), lambda i: (i, 0)),
            pl.BlockSpec((tm, HEAD_PAD), lambda i: (i, 0)),
        ],
        out_shape=[
            jax.ShapeDtypeStruct((QK_ROPE_DIM, n), f32),
            jax.ShapeDtypeStruct((QK_ROPE_DIM, n), f32),
            jax.ShapeDtypeStruct((n, HEAD_PAD), f32),
            jax.ShapeDtypeStruct((n, HEAD_PAD), f32),
        ],
        compiler_params=pltpu.CompilerParams(dimension_semantics=("parallel",)),
        name="rope_tables",
    )(positions.reshape(1, n), positions.reshape(n, 1), inv2.reshape(QK_ROPE_DIM, 1),
      inv_row.reshape(1, HEAD_PAD), mask_row.reshape(1, HEAD_PAD))


def _proj_kernel(x_ref, g_ref, win_ref, qn_ref, kvn_ref, wuq_t_ref, wk_ref, wuv_t_ref,
                 ws_ref, bias_ref, sgn_ref, cos_t_ref, sin_t_ref, cos_k_ref, sin_k_ref,
                 q_t_ref, k_ref, v_t_ref, hp_ref, c_ref):
    tm = x_ref.shape[0]
    scale = (QK_NOPE_DIM + QK_ROPE_DIM) ** -0.5 * LOG2_E
    h = _rms(x_ref[...], g_ref[...]).astype(bf16)
    p = _dot(h, win_ref[...])

    cq = _rms(p[:, C_Q:C_Q + Q_LORA_RANK], qn_ref[...]).astype(bf16)
    q_t = _dot_nt(wuq_t_ref[...], cq)
    cos_t = cos_t_ref[...]
    sin_t = sin_t_ref[...]
    for hd in range(MLA_HEADS):
        b0 = hd * HEAD_PAD
        nope = q_t[b0:b0 + QK_NOPE_DIM]
        rope = q_t[b0 + ROPE_OFF:b0 + ROPE_OFF + QK_ROPE_DIM]
        rot = q_t[b0 + ROPE_OFF + QK_ROPE_DIM:b0 + HEAD_PAD]
        q_t_ref[b0:b0 + QK_NOPE_DIM, :] = (nope * scale).astype(bf16)
        q_t_ref[b0 + ROPE_OFF:b0 + ROPE_OFF + QK_ROPE_DIM, :] = (
            (rope * cos_t + rot * sin_t) * scale).astype(bf16)
        q_t_ref[b0 + ROPE_OFF + QK_ROPE_DIM:b0 + HEAD_PAD, :] = jnp.zeros(
            (HEAD_PAD - ROPE_OFF - QK_ROPE_DIM, tm), bf16)

    ckv = _rms(p[:, C_KV:C_KV + KV_LORA_RANK], kvn_ref[...]).astype(bf16)
    k_nope = _dot(ckv, wk_ref[...])
    k_rope = (p[:, C_ROPE:C_ROPE + LANES] * cos_k_ref[...]
              + p[:, C_ROT:C_ROT + LANES] * sin_k_ref[...])
    for hd in range(MLA_HEADS):
        b0 = hd * HEAD_PAD
        k_ref[:, b0:b0 + HEAD_PAD] = (k_nope[:, b0:b0 + HEAD_PAD] + k_rope).astype(bf16)
    v_t_ref[...] = _dot_nt(wuv_t_ref[...], ckv).astype(bf16)

    hp_ref[...] = p[:, C_POOL:C_POOL + POOL_WIDTH]

    z = jax.nn.gelu(p[:, C_SG:C_SG + 2 * SG_WIDTH])
    u = z[:, :SG_WIDTH]
    v = z[:, SG_WIDTH:]
    grp = lax.broadcasted_iota(jnp.int32, (1, SG_WIDTH), 1) // SG_HEAD_DIM
    vsq = v * v
    inv = jnp.zeros_like(v)
    for g in range(SG_HEADS):
        sel = grp == g
        ms = jnp.sum(jnp.where(sel, vsq, 0.0), axis=-1, keepdims=True) / SG_HEAD_DIM
        inv = jnp.where(sel, lax.rsqrt(ms + EPS), inv)
    vn = (v * inv * sgn_ref[...]).astype(bf16)
    bias = bias_ref[...]
    for c in range(tm // SG_CHUNK):
        r0 = c * SG_CHUNK
        vc = vn[r0:r0 + SG_CHUNK]
        mixed = bias
        for g in range(SG_HEADS):
            mixed = mixed + jnp.where(grp == g, _dot(ws_ref[g], vc), 0.0)
        c_ref[r0:r0 + SG_CHUNK, :] = (u[r0:r0 + SG_CHUNK] * mixed).astype(bf16)


def _const_spec(shape):
    nd = len(shape)
    return pl.BlockSpec(shape, lambda *_: (0,) * nd)


def _proj_call(x2, w, tables, batch, seq):
    n = x2.shape[0]
    tm = TM_PROJ
    cos_t, sin_t, cos_k, sin_k = tables
    row = lambda cols: pl.BlockSpec((tm, cols), lambda i: (i, 0))
    col = lambda rows: pl.BlockSpec((rows, tm), lambda i: (0, i))
    tiles_per_seq = seq // tm
    q_spec = pl.BlockSpec((None, MLA_HEADS * HEAD_PAD, tm), lambda i: (i // tiles_per_seq, 0, i % tiles_per_seq))
    v_spec = pl.BlockSpec((None, MLA_WIDTH, tm), lambda i: (i // tiles_per_seq, 0, i % tiles_per_seq))
    k_spec = pl.BlockSpec((None, tm, MLA_HEADS * HEAD_PAD), lambda i: (i // tiles_per_seq, i % tiles_per_seq, 0))
    return pl.pallas_call(
        _proj_kernel,
        grid=(n // tm,),
        in_specs=[
            row(D_MODEL),
            _const_spec((1, D_MODEL)),
            _const_spec((D_MODEL, IN_COLS_PAD)),
            _const_spec((1, Q_LORA_RANK)),
            _const_spec((1, KV_LORA_RANK)),
            _const_spec((MLA_HEADS * HEAD_PAD, Q_LORA_RANK)),
            _const_spec((KV_LORA_RANK, MLA_HEADS * HEAD_PAD)),
            _const_spec((MLA_WIDTH, KV_LORA_RANK)),
            _const_spec((SG_HEADS, SG_CHUNK, SG_CHUNK)),
            _const_spec((SG_CHUNK, SG_WIDTH)),
            _const_spec((1, SG_WIDTH)),
            col(QK_ROPE_DIM),
            col(QK_ROPE_DIM),
            row(HEAD_PAD),
            row(HEAD_PAD),
        ],
        out_specs=[q_spec, k_spec, v_spec, row(POOL_WIDTH), row(SG_WIDTH)],
        out_shape=[
            jax.ShapeDtypeStruct((batch, MLA_HEADS * HEAD_PAD, seq), bf16),
            jax.ShapeDtypeStruct((batch, seq, MLA_HEADS * HEAD_PAD), bf16),
            jax.ShapeDtypeStruct((batch, MLA_WIDTH, seq), bf16),
            jax.ShapeDtypeStruct((n, POOL_WIDTH), f32),
            jax.ShapeDtypeStruct((n, SG_WIDTH), bf16),
        ],
        compiler_params=pltpu.CompilerParams(dimension_semantics=("parallel",), vmem_limit_bytes=VMEM_LIMIT),
        name="proj",
    )(x2, w["mix_norm"], w["w_in"], w["q_norm"], w["kv_norm"], w["w_uq_t"], w["w_k"], w["w_uv_t"],
      w["w_s"], w["sg_bias"], w["sg_norm"], cos_t, sin_t, cos_k, sin_k)


def _attn_kernel(q_t_ref, k_ref, v_t_ref, o_ref, s_buf, p_buf, *, tk):
    seq = k_ref.shape[0]
    tq = q_t_ref.shape[1]
    n_kv = seq // tk
    assert n_kv % 2 == 0 and n_kv >= 4
    outs = []
    for hh in range(2):
        q_t = q_t_ref[hh * HEAD_PAD:(hh + 1) * HEAD_PAD, :]

        def stage_a(t, slot):
            off = pl.multiple_of(t * tk, tk)
            s_t = _dot(k_ref[pl.ds(off, tk), hh * HEAD_PAD:(hh + 1) * HEAD_PAD], q_t)
            s_buf[slot] = s_t
            return jnp.max(s_t, axis=0, keepdims=True)

        def stage_b(slot, cmax, m, l):
            m_new = cmax if m is None else jnp.maximum(m, cmax)
            p_t = jnp.exp2(s_buf[slot] - m_new)
            p_buf[slot] = p_t.astype(bf16)
            psum = jnp.sum(p_t, axis=0, keepdims=True)
            if m is None:
                return m_new, psum, jnp.ones_like(psum)
            alpha = jnp.exp2(m - m_new)
            return m_new, alpha * l + psum, alpha

        def stage_c(t, slot, acc, alpha):
            off = pl.multiple_of(t * tk, tk)
            v_c = v_t_ref[hh * V_HEAD_DIM:(hh + 1) * V_HEAD_DIM, pl.ds(off, tk)]
            return alpha * acc + _dot(v_c, p_buf[slot])

        def tick(t, parity, state):
            cmax, m, l, alpha, acc = state
            cmax_new = stage_a(t, parity)
            m, l, alpha_new = stage_b(1 - parity, cmax, m, l)
            acc = stage_c(t - 2, parity, acc, alpha)
            return cmax_new, m, l, alpha_new, acc

        cmax = stage_a(0, 0)
        cmax_new = stage_a(1, 1)
        m, l, alpha = stage_b(0, cmax, None, None)
        state = (cmax_new, m, l, alpha, jnp.zeros((V_HEAD_DIM, tq), f32))

        for t in range(2, n_kv):
            state = tick(t, t % 2, state)
        cmax, m, l, alpha, acc = state
        m, l, alpha_new = stage_b(1, cmax, m, l)
        acc = stage_c(n_kv - 2, 0, acc, alpha)
        acc = stage_c(n_kv - 1, 1, acc, alpha_new)
        outs.append(acc / l)
    o_t = jnp.concatenate(outs, axis=0)
    o_ref[...] = o_t.T.astype(o_ref.dtype)


def _attn_call(q_t, k, v_t):
    batch, _, seq = q_t.shape
    pairs = MLA_HEADS // 2
    return pl.pallas_call(
        functools.partial(_attn_kernel, tk=TK),
        grid=(batch, pairs, seq // TQ),
        in_specs=[
            pl.BlockSpec((None, 2 * HEAD_PAD, TQ), lambda b, h, q: (b, h, q)),
            pl.BlockSpec((None, seq, 2 * HEAD_PAD), lambda b, h, q: (b, 0, h)),
            pl.BlockSpec((None, 2 * V_HEAD_DIM, seq), lambda b, h, q: (b, h, 0)),
        ],
        out_specs=pl.BlockSpec((None, TQ, 2 * V_HEAD_DIM), lambda b, h, q: (b, q, h)),
        out_shape=jax.ShapeDtypeStruct((batch, seq, MLA_WIDTH), bf16),
        scratch_shapes=[pltpu.VMEM((2, TK, TQ), f32), pltpu.VMEM((2, TK, TQ), bf16)],
        compiler_params=pltpu.CompilerParams(
            dimension_semantics=("parallel", "parallel", "arbitrary"), vmem_limit_bytes=VMEM_LIMIT),
        name="attn",
    )(q_t, k, v_t)


def _post_kernel(x_ref, a_ref, hp_ref, hp_prev_ref, hp_next_ref, c_ref, wo_ref, wpool_ref, pscale_ref,
                 fn_ref, wg_ref, wu_ref, wd_ref, final_ref, o_ref, ext_ref, *, seq, last):
    tm = x_ref.shape[0]
    tiles_per_seq = seq // tm
    si = pl.program_id(0) % tiles_per_seq
    hp = hp_ref[...]

    ext_ref[0:POOL_HALO, :] = jnp.where(si > 0, hp_prev_ref[...], 0.0)
    ext_ref[POOL_HALO:POOL_HALO + tm, :] = hp
    ext_ref[POOL_HALO + tm:2 * POOL_HALO + tm, :] = jnp.where(si < tiles_per_seq - 1, hp_next_ref[...], 0.0)

    def shifted(j):
        return ext_ref[POOL_HALO + j:POOL_HALO + j + tm, :]

    grp = lax.broadcasted_iota(jnp.int32, (1, POOL_WIDTH), 1) // POOL_GROUP_DIM
    t_abs = si * tm + lax.broadcasted_iota(jnp.int32, (tm, 1), 0)
    summed = jnp.zeros((tm, POOL_WIDTH), f32)
    cnt = jnp.zeros((tm, POOL_WIDTH), f32)
    lo_prev, hi_prev = 0, 0
    acc = None
    for g, w in enumerate(POOL_WINDOWS):
        left = w // 2
        right = w - 1 - left
        for j in list(range(-left, lo_prev)) + list(range(hi_prev, right + 1)):
            acc = shifted(j) if acc is None else acc + shifted(j)
        lo_prev, hi_prev = -left, right + 1
        n_in = (jnp.minimum(t_abs + right + 1, seq) - jnp.maximum(t_abs - left, 0)).astype(f32)
        summed = jnp.where(grp == g, acc, summed)
        cnt = jnp.where(grp == g, n_in, cnt)
    d = (summed / cnt - hp).astype(bf16)
    b = _dot(d, wpool_ref[...]) * pscale_ref[...]

    mixo = (_dot(a_ref[...], wo_ref[0:MLA_WIDTH, :])
            + _dot(b.astype(bf16), wo_ref[MLA_WIDTH:MLA_WIDTH + POOL_WIDTH, :])
            + _dot(c_ref[...], wo_ref[MLA_WIDTH + POOL_WIDTH:, :]))
    x1 = x_ref[...] + mixo

    hn = _rms(x1, fn_ref[...]).astype(bf16)
    y = None
    for f in range(D_FF // FF_CHUNK):
        f0 = f * FF_CHUNK
        gte = _dot(hn, wg_ref[:, f0:f0 + FF_CHUNK])
        up = _dot(hn, wu_ref[:, f0:f0 + FF_CHUNK])
        act = (jax.nn.silu(gte) * up).astype(bf16)
        part = _dot(act, wd_ref[f0:f0 + FF_CHUNK, :])
        y = part if y is None else y + part
    out = x1 + y
    if last:
        out = _rms(out, final_ref[...])
    o_ref[...] = out


def _post_call(x2, a2, hp, c, w, final_norm, seq, last):
    n = x2.shape[0]
    tm = TM_POST
    hb = tm // POOL_HALO
    n_halo_blocks = n // POOL_HALO
    row = lambda cols: pl.BlockSpec((tm, cols), lambda i: (i, 0))
    resident = lambda shape: pl.BlockSpec(shape, lambda i: (0,) * len(shape), pipeline_mode=pl.Buffered(1))
    return pl.pallas_call(
        functools.partial(_post_kernel, seq=seq, last=last),
        grid=(n // tm,),
        in_specs=[
            row(D_MODEL),
            row(MLA_WIDTH),
            row(POOL_WIDTH),
            pl.BlockSpec((POOL_HALO, POOL_WIDTH), lambda i: (jnp.maximum(i * hb - 1, 0), 0)),
            pl.BlockSpec((POOL_HALO, POOL_WIDTH), lambda i: (jnp.minimum((i + 1) * hb, n_halo_blocks - 1), 0)),
            row(SG_WIDTH),
            resident((D_MODEL, D_MODEL)),
            resident((POOL_WIDTH, POOL_WIDTH)),
            _const_spec((1, POOL_WIDTH)),
            _const_spec((1, D_MODEL)),
            resident((D_MODEL, D_FF)),
            resident((D_MODEL, D_FF)),
            resident((D_FF, D_MODEL)),
            _const_spec((1, D_MODEL)),
        ],
        out_specs=row(D_MODEL),
        out_shape=jax.ShapeDtypeStruct((n, D_MODEL), f32),
        scratch_shapes=[pltpu.VMEM((tm + 2 * POOL_HALO, POOL_WIDTH), f32)],
        compiler_params=pltpu.CompilerParams(dimension_semantics=("parallel",), vmem_limit_bytes=VMEM_LIMIT),
        name="post_last" if last else "post",
    )(x2, a2, hp, hp, hp, c, w["w_o"], w["w_pool"], w["pool_scale"], w["ffn_norm"],
      w["w_gate"], w["w_up"], w["w_down"], final_norm)


def _prep_layer(l, mix_norm, w_in, q_norm, kv_norm, w_uq, w_ukv, w_pool, pool_scale, sg_norm, w_s, b_s,
                w_o, ffn_norm, w_gate, w_up, w_down):
    o1 = Q_LORA_RANK
    o2 = o1 + KV_LORA_RANK
    o3 = o2 + QK_ROPE_DIM
    o4 = o3 + POOL_WIDTH
    half = QK_ROPE_DIM // 2

    def rot_cols(t):
        return jnp.concatenate([-t[..., half:], t[..., :half]], axis=-1)

    def pad_rope(t):
        z = jnp.zeros
        return jnp.concatenate([z((D_MODEL, ROPE_OFF), f32), t, z((D_MODEL, LANES - ROPE_OFF - QK_ROPE_DIM), f32)], axis=1)

    wi = w_in[l]
    w_kr = wi[:, o2:o3]
    w_in_pad = jnp.concatenate([wi[:, :o2], wi[:, o3:], pad_rope(w_kr), pad_rope(rot_cols(w_kr))], axis=1)

    wq = w_uq[l].reshape(Q_LORA_RANK, MLA_HEADS, QK_NOPE_DIM + QK_ROPE_DIM)
    wq_rope = wq[..., QK_NOPE_DIM:]
    wq_pad = jnp.concatenate([wq[..., :QK_NOPE_DIM], wq_rope, rot_cols(wq_rope)], axis=-1)
    w_uq_t = wq_pad.reshape(Q_LORA_RANK, MLA_HEADS * HEAD_PAD).T

    wkv = w_ukv[l].reshape(KV_LORA_RANK, MLA_HEADS, QK_NOPE_DIM + V_HEAD_DIM)
    w_k = jnp.concatenate([wkv[..., :QK_NOPE_DIM],
                           jnp.zeros((KV_LORA_RANK, MLA_HEADS, HEAD_PAD - QK_NOPE_DIM), f32)], axis=-1)
    w_k = w_k.reshape(KV_LORA_RANK, MLA_HEADS * HEAD_PAD)
    w_uv_t = wkv[..., QK_NOPE_DIM:].reshape(KV_LORA_RANK, MLA_WIDTH).T

    n_groups = len(POOL_WINDOWS)
    eye = jnp.eye(n_groups, dtype=f32)
    w_pool_bd = (eye[:, None, :, None] * w_pool[l][:, :, None, :]).reshape(POOL_WIDTH, POOL_WIDTH)

    sg_bias = jnp.repeat(b_s[l].T, SG_HEAD_DIM, axis=1)
    return dict(
        mix_norm=mix_norm[l].reshape(1, D_MODEL),
        w_in=w_in_pad.astype(bf16),
        q_norm=q_norm[l].reshape(1, Q_LORA_RANK),
        kv_norm=kv_norm[l].reshape(1, KV_LORA_RANK),
        w_uq_t=w_uq_t.astype(bf16),
        w_k=w_k.astype(bf16),
        w_uv_t=w_uv_t.astype(bf16),
        w_s=w_s[l].astype(bf16),
        sg_bias=sg_bias,
        sg_norm=sg_norm[l].reshape(1, SG_WIDTH),
        w_pool=w_pool_bd.astype(bf16),
        pool_scale=pool_scale[l].reshape(1, POOL_WIDTH),
        w_o=w_o[l].astype(bf16),
        ffn_norm=ffn_norm[l].reshape(1, D_MODEL),
        w_gate=w_gate[l].astype(bf16),
        w_up=w_up[l].astype(bf16),
        w_down=w_down[l].astype(bf16),
    )


def kernel(x, positions, mix_norm, w_in, q_norm, kv_norm, w_uq, w_ukv, w_pool, pool_scale, sg_norm, w_s, b_s,
           w_o, ffn_norm, w_gate, w_up, w_down, final_norm):
    batch, seq, _ = x.shape
    depth = w_in.shape[0]
    assert seq % TM_PROJ == 0 and seq % TM_POST == 0 and seq % TQ == 0 and seq % TK == 0
    tables = _rope_tables(positions)
    x2 = x.reshape(batch * seq, D_MODEL)
    fin = final_norm.reshape(1, D_MODEL)
    for l in range(depth):
        w = _prep_layer(l, mix_norm, w_in, q_norm, kv_norm, w_uq, w_ukv, w_pool, pool_scale, sg_norm, w_s, b_s,
                        w_o, ffn_norm, w_gate, w_up, w_down)
        q_t, k, v_t, hp, c = _proj_call(x2, w, tables, batch, seq)
        a = _attn_call(q_t, k, v_t)
        x2 = _post_call(x2, a.reshape(batch * seq, MLA_WIDTH), hp, c, w, fin, seq, last=(l == depth - 1))
    return x2.reshape(batch, seq, D_MODEL)
```

```python
import functools

import jax
import jax.numpy as jnp
from jax import lax
from jax.experimental import pallas as pl
from jax.experimental.pallas import tpu as pltpu

D_MODEL = 1024
MLA_HEADS = 8
QK_NOPE_DIM = 64
QK_ROPE_DIM = 32
V_HEAD_DIM = 64
Q_LORA_RANK = 384
KV_LORA_RANK = 256
ROPE_THETA = 10000.0
MLA_WIDTH = MLA_HEADS * V_HEAD_DIM
POOL_WINDOWS = (2, 4, 8, 16)
POOL_GROUP_DIM = 64
POOL_WIDTH = len(POOL_WINDOWS) * POOL_GROUP_DIM
SG_HEADS = 4
SG_HEAD_DIM = 64
SG_WIDTH = SG_HEADS * SG_HEAD_DIM
SG_CHUNK = 128
D_FF = 2816
EPS = 1e-6
LOG2_E = 1.4426950408889634

LANES = 128
HEAD_PAD = 128
ROPE_OFF = QK_NOPE_DIM
BF16_ROWS = 16
V_PAD = V_HEAD_DIM + BF16_ROWS
POOL_HALO = 8
FF_CHUNK = 256

C_Q = 0
C_KV = C_Q + Q_LORA_RANK
C_POOL = C_KV + KV_LORA_RANK
C_SG = C_POOL + POOL_WIDTH
C_ROPE = C_SG + 2 * SG_WIDTH
C_ROT = C_ROPE + LANES
IN_COLS_PAD = C_ROT + LANES

TM_PROJ = 512
TM_POST = 512
TQ = 512
TK = 256
TM_ROPE = 2048

VMEM_LIMIT = 56 * 1024 * 1024

f32 = jnp.float32
bf16 = jnp.bfloat16


def _rms(x, g):
    return x * lax.rsqrt(jnp.mean(x * x, axis=-1, keepdims=True) + EPS) * g


def _dot(a, b):
    return jnp.dot(a, b, preferred_element_type=f32)


def _dot_nt(a, b):
    return lax.dot_general(a, b, (((1,), (1,)), ((), ())), preferred_element_type=f32)


def _rope_kernel(pos_row_ref, pos_col_ref, inv_col_ref, inv_row_ref, mask_row_ref,
                 cos_t_ref, sin_t_ref, cos_k_ref, sin_k_ref):
    ang_t = pos_row_ref[...].astype(f32) * inv_col_ref[...]
    cos_t_ref[...] = jnp.cos(ang_t)
    sin_t_ref[...] = jnp.sin(ang_t)
    ang_k = pos_col_ref[...].astype(f32) * inv_row_ref[...]
    cos_k_ref[...] = jnp.cos(ang_k) * mask_row_ref[...]
    sin_k_ref[...] = jnp.sin(ang_k) * mask_row_ref[...]


def _rope_tables(positions):
    n = positions.size
    inv_freq = ROPE_THETA ** (-jnp.arange(0, QK_ROPE_DIM, 2, dtype=f32) / QK_ROPE_DIM)
    inv2 = jnp.concatenate([inv_freq, inv_freq])
    zeros = jnp.zeros
    inv_row = jnp.concatenate([zeros((ROPE_OFF,), f32), inv2, zeros((HEAD_PAD - ROPE_OFF - QK_ROPE_DIM,), f32)])
    mask_row = jnp.concatenate([zeros((ROPE_OFF,), f32), jnp.ones((QK_ROPE_DIM,), f32),
                                zeros((HEAD_PAD - ROPE_OFF - QK_ROPE_DIM,), f32)])
    tm = TM_ROPE
    return pl.pallas_call(
        _rope_kernel,
        grid=(n // tm,),
        in_specs=[
            pl.BlockSpec((1, tm), lambda i: (0, i)),
            pl.BlockSpec((tm, 1), lambda i: (i, 0)),
            pl.BlockSpec((QK_ROPE_DIM, 1), lambda i: (0, 0)),
            pl.BlockSpec((1, HEAD_PAD), lambda i: (0, 0)),
            pl.BlockSpec((1, HEAD_PAD), lambda i: (0, 0)),
        ],
        out_specs=[
            pl.BlockSpec((QK_ROPE_DIM, tm), lambda i: (0, i)),
            pl.BlockSpec((QK_ROPE_DIM, tm), lambda i: (0, i)),
            pl.BlockSpec((tm, HEAD_PAD), lambda i: (i, 0)),
            pl.BlockSpec((tm, HEAD_PAD), lambda i: (i, 0)),
        ],
        out_shape=[
            jax.ShapeDtypeStruct((QK_ROPE_DIM, n), f32),
            jax.ShapeDtypeStruct((QK_ROPE_DIM, n), f32),
            jax.ShapeDtypeStruct((n, HEAD_PAD), f32),
            jax.ShapeDtypeStruct((n, HEAD_PAD), f32),
        ],
        compiler_params=pltpu.CompilerParams(dimension_semantics=("parallel",)),
        name="rope_tables",
    )(positions.reshape(1, n), positions.reshape(n, 1), inv2.reshape(QK_ROPE_DIM, 1),
      inv_row.reshape(1, HEAD_PAD), mask_row.reshape(1, HEAD_PAD))


def _proj_kernel(x_ref, g_ref, win_ref, qn_ref, kvn_ref, wuq_t_ref, wk_ref, wuv_t_ref,
                 ws_ref, bias_ref, sgn_ref, cos_t_ref, sin_t_ref, cos_k_ref, sin_k_ref,
                 q_t_ref, k_ref, v_t_ref, hp_ref, c_ref):
    tm = x_ref.shape[0]
    scale = (QK_NOPE_DIM + QK_ROPE_DIM) ** -0.5 * LOG2_E
    h = _rms(x_ref[...], g_ref[...]).astype(bf16)
    p = _dot(h, win_ref[...])

    cq = _rms(p[:, C_Q:C_Q + Q_LORA_RANK], qn_ref[...]).astype(bf16)
    q_t = _dot_nt(wuq_t_ref[...], cq)
    cos_t = cos_t_ref[...]
    sin_t = sin_t_ref[...]
    for hd in range(MLA_HEADS):
        b0 = hd * HEAD_PAD
        nope = q_t[b0:b0 + QK_NOPE_DIM]
        rope = q_t[b0 + ROPE_OFF:b0 + ROPE_OFF + QK_ROPE_DIM]
        rot = q_t[b0 + ROPE_OFF + QK_ROPE_DIM:b0 + HEAD_PAD]
        q_t_ref[b0:b0 + QK_NOPE_DIM, :] = (nope * scale).astype(bf16)
        q_t_ref[b0 + ROPE_OFF:b0 + ROPE_OFF + QK_ROPE_DIM, :] = (
            (rope * cos_t + rot * sin_t) * scale).astype(bf16)
        q_t_ref[b0 + ROPE_OFF + QK_ROPE_DIM:b0 + HEAD_PAD, :] = jnp.zeros(
            (HEAD_PAD - ROPE_OFF - QK_ROPE_DIM, tm), bf16)

    ckv = _rms(p[:, C_KV:C_KV + KV_LORA_RANK], kvn_ref[...]).astype(bf16)
    k_nope = _dot(ckv, wk_ref[...])
    k_rope = (p[:, C_ROPE:C_ROPE + LANES] * cos_k_ref[...]
              + p[:, C_ROT:C_ROT + LANES] * sin_k_ref[...])
    for hd in range(MLA_HEADS):
        b0 = hd * HEAD_PAD
        k_ref[:, b0:b0 + HEAD_PAD] = (k_nope[:, b0:b0 + HEAD_PAD] + k_rope).astype(bf16)
    v_t = _dot_nt(wuv_t_ref[...], ckv).astype(bf16)
    for hd in range(MLA_HEADS):
        b0 = hd * V_PAD
        v_t_ref[b0:b0 + V_HEAD_DIM, :] = v_t[hd * V_HEAD_DIM:(hd + 1) * V_HEAD_DIM]
        v_t_ref[b0 + V_HEAD_DIM:b0 + V_PAD, :] = jnp.ones((V_PAD - V_HEAD_DIM, tm), bf16)

    hp_ref[...] = p[:, C_POOL:C_POOL + POOL_WIDTH]

    z = jax.nn.gelu(p[:, C_SG:C_SG + 2 * SG_WIDTH])
    u = z[:, :SG_WIDTH]
    v = z[:, SG_WIDTH:]
    grp = lax.broadcasted_iota(jnp.int32, (1, SG_WIDTH), 1) // SG_HEAD_DIM
    vsq = v * v
    inv = jnp.zeros_like(v)
    for g in range(SG_HEADS):
        sel = grp == g
        ms = jnp.sum(jnp.where(sel, vsq, 0.0), axis=-1, keepdims=True) / SG_HEAD_DIM
        inv = jnp.where(sel, lax.rsqrt(ms + EPS), inv)
    vn = (v * inv * sgn_ref[...]).astype(bf16)
    bias = bias_ref[...]
    for c in range(tm // SG_CHUNK):
        r0 = c * SG_CHUNK
        vc = vn[r0:r0 + SG_CHUNK]
        mixed = bias
        for g in range(SG_HEADS):
            mixed = mixed + jnp.where(grp == g, _dot(ws_ref[g], vc), 0.0)
        c_ref[r0:r0 + SG_CHUNK, :] = (u[r0:r0 + SG_CHUNK] * mixed).astype(bf16)


def _const_spec(shape):
    nd = len(shape)
    return pl.BlockSpec(shape, lambda *_: (0,) * nd)


def _proj_call(x2, w, tables, batch, seq):
    n = x2.shape[0]
    tm = TM_PROJ
    cos_t, sin_t, cos_k, sin_k = tables
    row = lambda cols: pl.BlockSpec((tm, cols), lambda i: (i, 0))
    col = lambda rows: pl.BlockSpec((rows, tm), lambda i: (0, i))
    tiles_per_seq = seq // tm
    q_spec = pl.BlockSpec((None, MLA_HEADS * HEAD_PAD, tm), lambda i: (i // tiles_per_seq, 0, i % tiles_per_seq))
    v_spec = pl.BlockSpec((None, MLA_HEADS * V_PAD, tm), lambda i: (i // tiles_per_seq, 0, i % tiles_per_seq))
    k_spec = pl.BlockSpec((None, tm, MLA_HEADS * HEAD_PAD), lambda i: (i // tiles_per_seq, i % tiles_per_seq, 0))
    return pl.pallas_call(
        _proj_kernel,
        grid=(n // tm,),
        in_specs=[
            row(D_MODEL),
            _const_spec((1, D_MODEL)),
            _const_spec((D_MODEL, IN_COLS_PAD)),
            _const_spec((1, Q_LORA_RANK)),
            _const_spec((1, KV_LORA_RANK)),
            _const_spec((MLA_HEADS * HEAD_PAD, Q_LORA_RANK)),
            _const_spec((KV_LORA_RANK, MLA_HEADS * HEAD_PAD)),
            _const_spec((MLA_WIDTH, KV_LORA_RANK)),
            _const_spec((SG_HEADS, SG_CHUNK, SG_CHUNK)),
            _const_spec((SG_CHUNK, SG_WIDTH)),
            _const_spec((1, SG_WIDTH)),
            col(QK_ROPE_DIM),
            col(QK_ROPE_DIM),
            row(HEAD_PAD),
            row(HEAD_PAD),
        ],
        out_specs=[q_spec, k_spec, v_spec, row(POOL_WIDTH), row(SG_WIDTH)],
        out_shape=[
            jax.ShapeDtypeStruct((batch, MLA_HEADS * HEAD_PAD, seq), bf16),
            jax.ShapeDtypeStruct((batch, seq, MLA_HEADS * HEAD_PAD), bf16),
            jax.ShapeDtypeStruct((batch, MLA_HEADS * V_PAD, seq), bf16),
            jax.ShapeDtypeStruct((n, POOL_WIDTH), f32),
            jax.ShapeDtypeStruct((n, SG_WIDTH), bf16),
        ],
        compiler_params=pltpu.CompilerParams(dimension_semantics=("parallel",), vmem_limit_bytes=VMEM_LIMIT),
        name="proj",
    )(x2, w["mix_norm"], w["w_in"], w["q_norm"], w["kv_norm"], w["w_uq_t"], w["w_k"], w["w_uv_t"],
      w["w_s"], w["sg_bias"], w["sg_norm"], cos_t, sin_t, cos_k, sin_k)


def _attn_kernel(q_t_ref, k_ref, v_t_ref, o_ref, s_buf, p_buf, acc_ref, *, tk):
    seq = k_ref.shape[0]
    n_kv = seq // tk
    assert n_kv >= 3
    outs = []
    for hh in range(2):
        q_t = q_t_ref[hh * HEAD_PAD:(hh + 1) * HEAD_PAD, :]

        def stage_a(t):
            s_t = _dot(k_ref[t * tk:(t + 1) * tk, hh * HEAD_PAD:(hh + 1) * HEAD_PAD], q_t)
            s_buf[t % 2] = s_t
            return jnp.max(s_t, axis=0, keepdims=True)

        def stage_b(t, cmax, m):
            m_new = cmax if m is None else jnp.maximum(m, cmax)
            p_buf[t % 2] = jnp.exp2(s_buf[t % 2] - m_new).astype(bf16)
            alpha = None if m is None else jnp.exp2(m - m_new)
            return m_new, alpha

        def stage_c(t, alpha):
            v_c = v_t_ref[hh * V_PAD:(hh + 1) * V_PAD, t * tk:(t + 1) * tk]
            part = _dot(v_c, p_buf[t % 2])
            acc_ref[...] = part if alpha is None else alpha * acc_ref[...] + part

        cmax, m, alphas = None, None, {}
        for t in range(n_kv + 2):
            cmax_new = stage_a(t) if t < n_kv else None
            if 1 <= t <= n_kv:
                m, alphas[t - 1] = stage_b(t - 1, cmax, m)
            if t >= 2:
                stage_c(t - 2, alphas.pop(t - 2))
            cmax = cmax_new
        acc = acc_ref[...]
        outs.append(acc[:V_HEAD_DIM] / acc[V_HEAD_DIM:V_HEAD_DIM + 1])
    o_t = jnp.concatenate(outs, axis=0)
    o_ref[...] = o_t.T.astype(o_ref.dtype)


def _attn_call(q_t, k, v_t):
    batch, _, seq = q_t.shape
    pairs = MLA_HEADS // 2
    return pl.pallas_call(
        functools.partial(_attn_kernel, tk=TK),
        grid=(batch, pairs, seq // TQ),
        in_specs=[
            pl.BlockSpec((None, 2 * HEAD_PAD, TQ), lambda b, h, q: (b, h, q)),
            pl.BlockSpec((None, seq, 2 * HEAD_PAD), lambda b, h, q: (b, 0, h)),
            pl.BlockSpec((None, 2 * V_PAD, seq), lambda b, h, q: (b, h, 0)),
        ],
        out_specs=pl.BlockSpec((None, TQ, 2 * V_HEAD_DIM), lambda b, h, q: (b, q, h)),
        out_shape=jax.ShapeDtypeStruct((batch, seq, MLA_WIDTH), bf16),
        scratch_shapes=[pltpu.VMEM((2, TK, TQ), f32), pltpu.VMEM((2, TK, TQ), bf16),
                        pltpu.VMEM((V_PAD, TQ), f32)],
        compiler_params=pltpu.CompilerParams(
            dimension_semantics=("parallel", "parallel", "arbitrary"), vmem_limit_bytes=VMEM_LIMIT),
        name="attn",
    )(q_t, k, v_t)


def _post_kernel(x_ref, a_ref, hp_ref, hp_prev_ref, hp_next_ref, c_ref, wo_ref, wpool_ref, pscale_ref,
                 fn_ref, wg_ref, wu_ref, wd_ref, final_ref, o_ref, ext_ref, *, seq, last):
    tm = x_ref.shape[0]
    tiles_per_seq = seq // tm
    si = pl.program_id(0) % tiles_per_seq
    hp = hp_ref[...]

    ext_ref[0:POOL_HALO, :] = jnp.where(si > 0, hp_prev_ref[...], 0.0)
    ext_ref[POOL_HALO:POOL_HALO + tm, :] = hp
    ext_ref[POOL_HALO + tm:2 * POOL_HALO + tm, :] = jnp.where(si < tiles_per_seq - 1, hp_next_ref[...], 0.0)

    def shifted(j):
        return ext_ref[POOL_HALO + j:POOL_HALO + j + tm, :]

    grp = lax.broadcasted_iota(jnp.int32, (1, POOL_WIDTH), 1) // POOL_GROUP_DIM
    t_abs = si * tm + lax.broadcasted_iota(jnp.int32, (tm, 1), 0)
    summed = jnp.zeros((tm, POOL_WIDTH), f32)
    cnt = jnp.zeros((tm, POOL_WIDTH), f32)
    lo_prev, hi_prev = 0, 0
    acc = None
    for g, w in enumerate(POOL_WINDOWS):
        left = w // 2
        right = w - 1 - left
        for j in list(range(-left, lo_prev)) + list(range(hi_prev, right + 1)):
            acc = shifted(j) if acc is None else acc + shifted(j)
        lo_prev, hi_prev = -left, right + 1
        n_in = (jnp.minimum(t_abs + right + 1, seq) - jnp.maximum(t_abs - left, 0)).astype(f32)
        summed = jnp.where(grp == g, acc, summed)
        cnt = jnp.where(grp == g, n_in, cnt)
    d = (summed / cnt - hp).astype(bf16)
    b = _dot(d, wpool_ref[...]) * pscale_ref[...]

    mixo = (_dot(a_ref[...], wo_ref[0:MLA_WIDTH, :])
            + _dot(b.astype(bf16), wo_ref[MLA_WIDTH:MLA_WIDTH + POOL_WIDTH, :])
            + _dot(c_ref[...], wo_ref[MLA_WIDTH + POOL_WIDTH:, :]))
    x1 = x_ref[...] + mixo

    hn = _rms(x1, fn_ref[...]).astype(bf16)
    y = None
    for f in range(D_FF // FF_CHUNK):
        f0 = f * FF_CHUNK
        gte = _dot(hn, wg_ref[:, f0:f0 + FF_CHUNK])
        up = _dot(hn, wu_ref[:, f0:f0 + FF_CHUNK])
        act = (jax.nn.silu(gte) * up).astype(bf16)
        part = _dot(act, wd_ref[f0:f0 + FF_CHUNK, :])
        y = part if y is None else y + part
    out = x1 + y
    if last:
        out = _rms(out, final_ref[...])
    o_ref[...] = out


def _post_call(x2, a2, hp, c, w, final_norm, seq, last):
    n = x2.shape[0]
    tm = TM_POST
    hb = tm // POOL_HALO
    n_halo_blocks = n // POOL_HALO
    row = lambda cols: pl.BlockSpec((tm, cols), lambda i: (i, 0))
    resident = lambda shape: pl.BlockSpec(shape, lambda i: (0,) * len(shape), pipeline_mode=pl.Buffered(1))
    return pl.pallas_call(
        functools.partial(_post_kernel, seq=seq, last=last),
        grid=(n // tm,),
        in_specs=[
            row(D_MODEL),
            row(MLA_WIDTH),
            row(POOL_WIDTH),
            pl.BlockSpec((POOL_HALO, POOL_WIDTH), lambda i: (jnp.maximum(i * hb - 1, 0), 0)),
            pl.BlockSpec((POOL_HALO, POOL_WIDTH), lambda i: (jnp.minimum((i + 1) * hb, n_halo_blocks - 1), 0)),
            row(SG_WIDTH),
            resident((D_MODEL, D_MODEL)),
            resident((POOL_WIDTH, POOL_WIDTH)),
            _const_spec((1, POOL_WIDTH)),
            _const_spec((1, D_MODEL)),
            resident((D_MODEL, D_FF)),
            resident((D_MODEL, D_FF)),
            resident((D_FF, D_MODEL)),
            _const_spec((1, D_MODEL)),
        ],
        out_specs=row(D_MODEL),
        out_shape=jax.ShapeDtypeStruct((n, D_MODEL), f32),
        scratch_shapes=[pltpu.VMEM((tm + 2 * POOL_HALO, POOL_WIDTH), f32)],
        compiler_params=pltpu.CompilerParams(dimension_semantics=("parallel",), vmem_limit_bytes=VMEM_LIMIT),
        name="post_last" if last else "post",
    )(x2, a2, hp, hp, hp, c, w["w_o"], w["w_pool"], w["pool_scale"], w["ffn_norm"],
      w["w_gate"], w["w_up"], w["w_down"], final_norm)


def _prep_layer(l, mix_norm, w_in, q_norm, kv_norm, w_uq, w_ukv, w_pool, pool_scale, sg_norm, w_s, b_s,
                w_o, ffn_norm, w_gate, w_up, w_down):
    o1 = Q_LORA_RANK
    o2 = o1 + KV_LORA_RANK
    o3 = o2 + QK_ROPE_DIM
    o4 = o3 + POOL_WIDTH
    half = QK_ROPE_DIM // 2

    def rot_cols(t):
        return jnp.concatenate([-t[..., half:], t[..., :half]], axis=-1)

    def pad_rope(t):
        z = jnp.zeros
        return jnp.concatenate([z((D_MODEL, ROPE_OFF), f32), t, z((D_MODEL, LANES - ROPE_OFF - QK_ROPE_DIM), f32)], axis=1)

    wi = w_in[l]
    w_kr = wi[:, o2:o3]
    w_in_pad = jnp.concatenate([wi[:, :o2], wi[:, o3:], pad_rope(w_kr), pad_rope(rot_cols(w_kr))], axis=1)

    wq = w_uq[l].reshape(Q_LORA_RANK, MLA_HEADS, QK_NOPE_DIM + QK_ROPE_DIM)
    wq_rope = wq[..., QK_NOPE_DIM:]
    wq_pad = jnp.concatenate([wq[..., :QK_NOPE_DIM], wq_rope, rot_cols(wq_rope)], axis=-1)
    w_uq_t = wq_pad.reshape(Q_LORA_RANK, MLA_HEADS * HEAD_PAD).T

    wkv = w_ukv[l].reshape(KV_LORA_RANK, MLA_HEADS, QK_NOPE_DIM + V_HEAD_DIM)
    w_k = jnp.concatenate([wkv[..., :QK_NOPE_DIM],
                           jnp.zeros((KV_LORA_RANK, MLA_HEADS, HEAD_PAD - QK_NOPE_DIM), f32)], axis=-1)
    w_k = w_k.reshape(KV_LORA_RANK, MLA_HEADS * HEAD_PAD)
    w_uv_t = wkv[..., QK_NOPE_DIM:].reshape(KV_LORA_RANK, MLA_WIDTH).T

    n_groups = len(POOL_WINDOWS)
    eye = jnp.eye(n_groups, dtype=f32)
    w_pool_bd = (eye[:, None, :, None] * w_pool[l][:, :, None, :]).reshape(POOL_WIDTH, POOL_WIDTH)

    sg_bias = jnp.repeat(b_s[l].T, SG_HEAD_DIM, axis=1)
    return dict(
        mix_norm=mix_norm[l].reshape(1, D_MODEL),
        w_in=w_in_pad.astype(bf16),
        q_norm=q_norm[l].reshape(1, Q_LORA_RANK),
        kv_norm=kv_norm[l].reshape(1, KV_LORA_RANK),
        w_uq_t=w_uq_t.astype(bf16),
        w_k=w_k.astype(bf16),
        w_uv_t=w_uv_t.astype(bf16),
        w_s=w_s[l].astype(bf16),
        sg_bias=sg_bias,
        sg_norm=sg_norm[l].reshape(1, SG_WIDTH),
        w_pool=w_pool_bd.astype(bf16),
        pool_scale=pool_scale[l].reshape(1, POOL_WIDTH),
        w_o=w_o[l].astype(bf16),
        ffn_norm=ffn_norm[l].reshape(1, D_MODEL),
        w_gate=w_gate[l].astype(bf16),
        w_up=w_up[l].astype(bf16),
        w_down=w_down[l].astype(bf16),
    )


def kernel(x, positions, mix_norm, w_in, q_norm, kv_norm, w_uq, w_ukv, w_pool, pool_scale, sg_norm, w_s, b_s,
           w_o, ffn_norm, w_gate, w_up, w_down, final_norm):
    batch, seq, _ = x.shape
    depth = w_in.shape[0]
    assert seq % TM_PROJ == 0 and seq % TM_POST == 0 and seq % TQ == 0 and seq % TK == 0
    tables = _rope_tables(positions)
    x2 = x.reshape(batch * seq, D_MODEL)
    fin = final_norm.reshape(1, D_MODEL)
    for l in range(depth):
        w = _prep_layer(l, mix_norm, w_in, q_norm, kv_norm, w_uq, w_ukv, w_pool, pool_scale, sg_norm, w_s, b_s,
                        w_o, ffn_norm, w_gate, w_up, w_down)
        q_t, k, v_t, hp, c = _proj_call(x2, w, tables, batch, seq)
        a = _attn_call(q_t, k, v_t)
        x2 = _post_call(x2, a.reshape(batch * seq, MLA_WIDTH), hp, c, w, fin, seq, last=(l == depth - 1))
    return x2.reshape(batch, seq, D_MODEL)
```

```python
import functools

import jax
import jax.numpy as jnp
from jax import lax
from jax.experimental import pallas as pl
from jax.experimental.pallas import tpu as pltpu

D_MODEL = 1024
MLA_HEADS = 8
QK_NOPE_DIM = 64
QK_ROPE_DIM = 32
V_HEAD_DIM = 64
Q_LORA_RANK = 384
KV_LORA_RANK = 256
ROPE_THETA = 10000.0
MLA_WIDTH = MLA_HEADS * V_HEAD_DIM
POOL_WINDOWS = (2, 4, 8, 16)
POOL_GROUP_DIM = 64
POOL_WIDTH = len(POOL_WINDOWS) * POOL_GROUP_DIM
SG_HEADS = 4
SG_HEAD_DIM = 64
SG_WIDTH = SG_HEADS * SG_HEAD_DIM
SG_CHUNK = 128
D_FF = 2816
EPS = 1e-6
LOG2_E = 1.4426950408889634

LANES = 128
HEAD_PAD = 128
ROPE_OFF = QK_NOPE_DIM
BF16_ROWS = 16
V_PAD = V_HEAD_DIM + BF16_ROWS
POOL_HALO = 8
FF_CHUNK = 256

C_Q = 0
C_KV = C_Q + Q_LORA_RANK
C_POOL = C_KV + KV_LORA_RANK
C_SG = C_POOL + POOL_WIDTH
C_ROPE = C_SG + 2 * SG_WIDTH
C_ROT = C_ROPE + LANES
IN_COLS_PAD = C_ROT + LANES

TM_PROJ = 512
TM_POST = 512
TQ = 512
TK = 256
Q_SUBTILES = 2
TM_ROPE = 2048

VMEM_LIMIT = 56 * 1024 * 1024

f32 = jnp.float32
bf16 = jnp.bfloat16


def _rms(x, g):
    return x * lax.rsqrt(jnp.mean(x * x, axis=-1, keepdims=True) + EPS) * g


def _dot(a, b):
    return jnp.dot(a, b, preferred_element_type=f32)


def _dot_nt(a, b):
    return lax.dot_general(a, b, (((1,), (1,)), ((), ())), preferred_element_type=f32)


def _rope_kernel(pos_row_ref, pos_col_ref, inv_col_ref, inv_row_ref, mask_row_ref,
                 cos_t_ref, sin_t_ref, cos_k_ref, sin_k_ref):
    ang_t = pos_row_ref[...].astype(f32) * inv_col_ref[...]
    cos_t_ref[...] = jnp.cos(ang_t)
    sin_t_ref[...] = jnp.sin(ang_t)
    ang_k = pos_col_ref[...].astype(f32) * inv_row_ref[...]
    cos_k_ref[...] = jnp.cos(ang_k) * mask_row_ref[...]
    sin_k_ref[...] = jnp.sin(ang_k) * mask_row_ref[...]


def _rope_tables(positions):
    n = positions.size
    inv_freq = ROPE_THETA ** (-jnp.arange(0, QK_ROPE_DIM, 2, dtype=f32) / QK_ROPE_DIM)
    inv2 = jnp.concatenate([inv_freq, inv_freq])
    zeros = jnp.zeros
    inv_row = jnp.concatenate([zeros((ROPE_OFF,), f32), inv2, zeros((HEAD_PAD - ROPE_OFF - QK_ROPE_DIM,), f32)])
    mask_row = jnp.concatenate([zeros((ROPE_OFF,), f32), jnp.ones((QK_ROPE_DIM,), f32),
                                zeros((HEAD_PAD - ROPE_OFF - QK_ROPE_DIM,), f32)])
    tm = TM_ROPE
    return pl.pallas_call(
        _rope_kernel,
        grid=(n // tm,),
        in_specs=[
            pl.BlockSpec((1, tm), lambda i: (0, i)),
            pl.BlockSpec((tm, 1), lambda i: (i, 0)),
            pl.BlockSpec((QK_ROPE_DIM, 1), lambda i: (0, 0)),
            pl.BlockSpec((1, HEAD_PAD), lambda i: (0, 0)),
            pl.BlockSpec((1, HEAD_PAD), lambda i: (0, 0)),
        ],
        out_specs=[
            pl.BlockSpec((QK_ROPE_DIM, tm), lambda i: (0, i)),
            pl.BlockSpec((QK_ROPE_DIM, tm), lambda i: (0, i)),
            pl.BlockSpec((tm, HEAD_PAD), lambda i: (i, 0)),
            pl.BlockSpec((tm, HEAD_PAD), lambda i: (i, 0)),
        ],
        out_shape=[
            jax.ShapeDtypeStruct((QK_ROPE_DIM, n), f32),
            jax.ShapeDtypeStruct((QK_ROPE_DIM, n), f32),
            jax.ShapeDtypeStruct((n, HEAD_PAD), f32),
            jax.ShapeDtypeStruct((n, HEAD_PAD), f32),
        ],
        compiler_params=pltpu.CompilerParams(dimension_semantics=("parallel",)),
        name="rope_tables",
    )(positions.reshape(1, n), positions.reshape(n, 1), inv2.reshape(QK_ROPE_DIM, 1),
      inv_row.reshape(1, HEAD_PAD), mask_row.reshape(1, HEAD_PAD))


def _proj_kernel(x_ref, g_ref, win_ref, qn_ref, kvn_ref, wuq_t_ref, wk_ref, wuv_t_ref,
                 ws_ref, bias_ref, sgn_ref, cos_t_ref, sin_t_ref, cos_k_ref, sin_k_ref,
                 q_t_ref, k_ref, v_t_ref, hp_ref, c_ref):
    tm = x_ref.shape[0]
    scale = (QK_NOPE_DIM + QK_ROPE_DIM) ** -0.5 * LOG2_E
    h = _rms(x_ref[...], g_ref[...]).astype(bf16)
    p = _dot(h, win_ref[...])

    cq = _rms(p[:, C_Q:C_Q + Q_LORA_RANK], qn_ref[...]).astype(bf16)
    q_t = _dot_nt(wuq_t_ref[...], cq)
    cos_t = cos_t_ref[...]
    sin_t = sin_t_ref[...]
    for hd in range(MLA_HEADS):
        b0 = hd * HEAD_PAD
        nope = q_t[b0:b0 + QK_NOPE_DIM]
        rope = q_t[b0 + ROPE_OFF:b0 + ROPE_OFF + QK_ROPE_DIM]
        rot = q_t[b0 + ROPE_OFF + QK_ROPE_DIM:b0 + HEAD_PAD]
        q_t_ref[b0:b0 + QK_NOPE_DIM, :] = (nope * scale).astype(bf16)
        q_t_ref[b0 + ROPE_OFF:b0 + ROPE_OFF + QK_ROPE_DIM, :] = (
            (rope * cos_t + rot * sin_t) * scale).astype(bf16)
        q_t_ref[b0 + ROPE_OFF + QK_ROPE_DIM:b0 + HEAD_PAD, :] = jnp.zeros(
            (HEAD_PAD - ROPE_OFF - QK_ROPE_DIM, tm), bf16)

    ckv = _rms(p[:, C_KV:C_KV + KV_LORA_RANK], kvn_ref[...]).astype(bf16)
    k_nope = _dot(ckv, wk_ref[...])
    k_rope = (p[:, C_ROPE:C_ROPE + LANES] * cos_k_ref[...]
              + p[:, C_ROT:C_ROT + LANES] * sin_k_ref[...])
    for hd in range(MLA_HEADS):
        b0 = hd * HEAD_PAD
        k_ref[:, b0:b0 + HEAD_PAD] = (k_nope[:, b0:b0 + HEAD_PAD] + k_rope).astype(bf16)
    v_t = _dot_nt(wuv_t_ref[...], ckv).astype(bf16)
    for hd in range(MLA_HEADS):
        b0 = hd * V_PAD
        v_t_ref[b0:b0 + V_HEAD_DIM, :] = v_t[hd * V_HEAD_DIM:(hd + 1) * V_HEAD_DIM]
        v_t_ref[b0 + V_HEAD_DIM:b0 + V_PAD, :] = jnp.ones((V_PAD - V_HEAD_DIM, tm), bf16)

    hp_ref[...] = p[:, C_POOL:C_POOL + POOL_WIDTH]

    z = jax.nn.gelu(p[:, C_SG:C_SG + 2 * SG_WIDTH])
    u = z[:, :SG_WIDTH]
    v = z[:, SG_WIDTH:]
    grp = lax.broadcasted_iota(jnp.int32, (1, SG_WIDTH), 1) // SG_HEAD_DIM
    vsq = v * v
    inv = jnp.zeros_like(v)
    for g in range(SG_HEADS):
        sel = grp == g
        ms = jnp.sum(jnp.where(sel, vsq, 0.0), axis=-1, keepdims=True) / SG_HEAD_DIM
        inv = jnp.where(sel, lax.rsqrt(ms + EPS), inv)
    vn = (v * inv * sgn_ref[...]).astype(bf16)
    bias = bias_ref[...]
    for c in range(tm // SG_CHUNK):
        r0 = c * SG_CHUNK
        vc = vn[r0:r0 + SG_CHUNK]
        mixed = bias
        for g in range(SG_HEADS):
            mixed = mixed + jnp.where(grp == g, _dot(ws_ref[g], vc), 0.0)
        c_ref[r0:r0 + SG_CHUNK, :] = (u[r0:r0 + SG_CHUNK] * mixed).astype(bf16)


def _const_spec(shape):
    nd = len(shape)
    return pl.BlockSpec(shape, lambda *_: (0,) * nd)


def _proj_call(x2, w, tables, batch, seq):
    n = x2.shape[0]
    tm = TM_PROJ
    cos_t, sin_t, cos_k, sin_k = tables
    row = lambda cols: pl.BlockSpec((tm, cols), lambda i: (i, 0))
    col = lambda rows: pl.BlockSpec((rows, tm), lambda i: (0, i))
    tiles_per_seq = seq // tm
    q_spec = pl.BlockSpec((None, MLA_HEADS * HEAD_PAD, tm), lambda i: (i // tiles_per_seq, 0, i % tiles_per_seq))
    v_spec = pl.BlockSpec((None, MLA_HEADS * V_PAD, tm), lambda i: (i // tiles_per_seq, 0, i % tiles_per_seq))
    k_spec = pl.BlockSpec((None, tm, MLA_HEADS * HEAD_PAD), lambda i: (i // tiles_per_seq, i % tiles_per_seq, 0))
    return pl.pallas_call(
        _proj_kernel,
        grid=(n // tm,),
        in_specs=[
            row(D_MODEL),
            _const_spec((1, D_MODEL)),
            _const_spec((D_MODEL, IN_COLS_PAD)),
            _const_spec((1, Q_LORA_RANK)),
            _const_spec((1, KV_LORA_RANK)),
            _const_spec((MLA_HEADS * HEAD_PAD, Q_LORA_RANK)),
            _const_spec((KV_LORA_RANK, MLA_HEADS * HEAD_PAD)),
            _const_spec((MLA_WIDTH, KV_LORA_RANK)),
            _const_spec((SG_HEADS, SG_CHUNK, SG_CHUNK)),
            _const_spec((SG_CHUNK, SG_WIDTH)),
            _const_spec((1, SG_WIDTH)),
            col(QK_ROPE_DIM),
            col(QK_ROPE_DIM),
            row(HEAD_PAD),
            row(HEAD_PAD),
        ],
        out_specs=[q_spec, k_spec, v_spec, row(POOL_WIDTH), row(SG_WIDTH)],
        out_shape=[
            jax.ShapeDtypeStruct((batch, MLA_HEADS * HEAD_PAD, seq), bf16),
            jax.ShapeDtypeStruct((batch, seq, MLA_HEADS * HEAD_PAD), bf16),
            jax.ShapeDtypeStruct((batch, MLA_HEADS * V_PAD, seq), bf16),
            jax.ShapeDtypeStruct((n, POOL_WIDTH), f32),
            jax.ShapeDtypeStruct((n, SG_WIDTH), bf16),
        ],
        compiler_params=pltpu.CompilerParams(dimension_semantics=("parallel",), vmem_limit_bytes=VMEM_LIMIT),
        name="proj",
    )(x2, w["mix_norm"], w["w_in"], w["q_norm"], w["kv_norm"], w["w_uq_t"], w["w_k"], w["w_uv_t"],
      w["w_s"], w["sg_bias"], w["sg_norm"], cos_t, sin_t, cos_k, sin_k)


def _attn_kernel(q_t_ref, k_ref, v_t_ref, o_ref, s_buf, p_buf, acc_ref, *, tq, tk):
    seq = k_ref.shape[0]
    n_kv = seq // tk
    assert n_kv >= 3
    n_sub = q_t_ref.shape[1] // tq

    def stream(sid, qs, hh):
        q_t = q_t_ref[hh * HEAD_PAD:(hh + 1) * HEAD_PAD, qs * tq:(qs + 1) * tq]

        def stage_a(t):
            s_t = _dot(k_ref[t * tk:(t + 1) * tk, hh * HEAD_PAD:(hh + 1) * HEAD_PAD], q_t)
            s_buf[sid, t % 2] = s_t
            return jnp.max(s_t, axis=0, keepdims=True)

        def stage_b(t, cmax, m):
            m_new = cmax if m is None else jnp.maximum(m, cmax)
            p_buf[sid, t % 2] = jnp.exp2(s_buf[sid, t % 2] - m_new).astype(bf16)
            alpha = None if m is None else jnp.exp2(m - m_new)
            return m_new, alpha

        def stage_c(t, alpha):
            v_c = v_t_ref[hh * V_PAD:(hh + 1) * V_PAD, t * tk:(t + 1) * tk]
            part = _dot(v_c, p_buf[sid, t % 2])
            acc_ref[sid] = part if alpha is None else alpha * acc_ref[sid] + part

        cmax, m, alphas = None, None, {}
        for t in range(n_kv + 2):
            cmax_new = stage_a(t) if t < n_kv else None
            if 1 <= t <= n_kv:
                m, alphas[t - 1] = stage_b(t - 1, cmax, m)
            if t >= 2:
                stage_c(t - 2, alphas.pop(t - 2))
            cmax = cmax_new
        acc = acc_ref[sid]
        return acc[:V_HEAD_DIM] / acc[V_HEAD_DIM:V_HEAD_DIM + 1]

    for qs in range(n_sub):
        o_t = jnp.concatenate([stream(2 * qs + hh, qs, hh) for hh in range(2)], axis=0)
        o_ref[qs * tq:(qs + 1) * tq, :] = o_t.T.astype(o_ref.dtype)


def _attn_call(q_t, k, v_t):
    batch, _, seq = q_t.shape
    pairs = MLA_HEADS // 2
    n_streams = 2 * Q_SUBTILES
    tq_blk = TQ * Q_SUBTILES
    return pl.pallas_call(
        functools.partial(_attn_kernel, tq=TQ, tk=TK),
        grid=(batch, pairs, seq // tq_blk),
        in_specs=[
            pl.BlockSpec((None, 2 * HEAD_PAD, tq_blk), lambda b, h, q: (b, h, q)),
            pl.BlockSpec((None, seq, 2 * HEAD_PAD), lambda b, h, q: (b, 0, h)),
            pl.BlockSpec((None, 2 * V_PAD, seq), lambda b, h, q: (b, h, 0)),
        ],
        out_specs=pl.BlockSpec((None, tq_blk, 2 * V_HEAD_DIM), lambda b, h, q: (b, q, h)),
        out_shape=jax.ShapeDtypeStruct((batch, seq, MLA_WIDTH), bf16),
        scratch_shapes=[pltpu.VMEM((n_streams, 2, TK, TQ), f32), pltpu.VMEM((n_streams, 2, TK, TQ), bf16),
                        pltpu.VMEM((n_streams, V_PAD, TQ), f32)],
        compiler_params=pltpu.CompilerParams(
            dimension_semantics=("parallel", "parallel", "arbitrary"), vmem_limit_bytes=VMEM_LIMIT),
        name="attn",
    )(q_t, k, v_t)


def _post_kernel(x_ref, a_ref, hp_ref, hp_prev_ref, hp_next_ref, c_ref, wo_ref, wpool_ref, pscale_ref,
                 fn_ref, wg_ref, wu_ref, wd_ref, final_ref, o_ref, ext_ref, *, seq, last):
    tm = x_ref.shape[0]
    tiles_per_seq = seq // tm
    si = pl.program_id(0) % tiles_per_seq
    hp = hp_ref[...]

    ext_ref[0:POOL_HALO, :] = jnp.where(si > 0, hp_prev_ref[...], 0.0)
    ext_ref[POOL_HALO:POOL_HALO + tm, :] = hp
    ext_ref[POOL_HALO + tm:2 * POOL_HALO + tm, :] = jnp.where(si < tiles_per_seq - 1, hp_next_ref[...], 0.0)

    def shifted(j):
        return ext_ref[POOL_HALO + j:POOL_HALO + j + tm, :]

    grp = lax.broadcasted_iota(jnp.int32, (1, POOL_WIDTH), 1) // POOL_GROUP_DIM
    t_abs = si * tm + lax.broadcasted_iota(jnp.int32, (tm, 1), 0)
    summed = jnp.zeros((tm, POOL_WIDTH), f32)
    cnt = jnp.zeros((tm, POOL_WIDTH), f32)
    lo_prev, hi_prev = 0, 0
    acc = None
    for g, w in enumerate(POOL_WINDOWS):
        left = w // 2
        right = w - 1 - left
        for j in list(range(-left, lo_prev)) + list(range(hi_prev, right + 1)):
            acc = shifted(j) if acc is None else acc + shifted(j)
        lo_prev, hi_prev = -left, right + 1
        n_in = (jnp.minimum(t_abs + right + 1, seq) - jnp.maximum(t_abs - left, 0)).astype(f32)
        summed = jnp.where(grp == g, acc, summed)
        cnt = jnp.where(grp == g, n_in, cnt)
    d = (summed / cnt - hp).astype(bf16)
    b = _dot(d, wpool_ref[...]) * pscale_ref[...]

    mixo = (_dot(a_ref[...], wo_ref[0:MLA_WIDTH, :])
            + _dot(b.astype(bf16), wo_ref[MLA_WIDTH:MLA_WIDTH + POOL_WIDTH, :])
            + _dot(c_ref[...], wo_ref[MLA_WIDTH + POOL_WIDTH:, :]))
    x1 = x_ref[...] + mixo

    hn = _rms(x1, fn_ref[...]).astype(bf16)
    y = None
    for f in range(D_FF // FF_CHUNK):
        f0 = f * FF_CHUNK
        gte = _dot(hn, wg_ref[:, f0:f0 + FF_CHUNK])
        up = _dot(hn, wu_ref[:, f0:f0 + FF_CHUNK])
        act = (jax.nn.silu(gte) * up).astype(bf16)
        part = _dot(act, wd_ref[f0:f0 + FF_CHUNK, :])
        y = part if y is None else y + part
    out = x1 + y
    if last:
        out = _rms(out, final_ref[...])
    o_ref[...] = out


def _post_call(x2, a2, hp, c, w, final_norm, seq, last):
    n = x2.shape[0]
    tm = TM_POST
    hb = tm // POOL_HALO
    n_halo_blocks = n // POOL_HALO
    row = lambda cols: pl.BlockSpec((tm, cols), lambda i: (i, 0))
    resident = lambda shape: pl.BlockSpec(shape, lambda i: (0,) * len(shape), pipeline_mode=pl.Buffered(1))
    return pl.pallas_call(
        functools.partial(_post_kernel, seq=seq, last=last),
        grid=(n // tm,),
        in_specs=[
            row(D_MODEL),
            row(MLA_WIDTH),
            row(POOL_WIDTH),
            pl.BlockSpec((POOL_HALO, POOL_WIDTH), lambda i: (jnp.maximum(i * hb - 1, 0), 0)),
            pl.BlockSpec((POOL_HALO, POOL_WIDTH), lambda i: (jnp.minimum((i + 1) * hb, n_halo_blocks - 1), 0)),
            row(SG_WIDTH),
            resident((D_MODEL, D_MODEL)),
            resident((POOL_WIDTH, POOL_WIDTH)),
            _const_spec((1, POOL_WIDTH)),
            _const_spec((1, D_MODEL)),
            resident((D_MODEL, D_FF)),
            resident((D_MODEL, D_FF)),
            resident((D_FF, D_MODEL)),
            _const_spec((1, D_MODEL)),
        ],
        out_specs=row(D_MODEL),
        out_shape=jax.ShapeDtypeStruct((n, D_MODEL), f32),
        scratch_shapes=[pltpu.VMEM((tm + 2 * POOL_HALO, POOL_WIDTH), f32)],
        compiler_params=pltpu.CompilerParams(dimension_semantics=("parallel",), vmem_limit_bytes=VMEM_LIMIT),
        name="post_last" if last else "post",
    )(x2, a2, hp, hp, hp, c, w["w_o"], w["w_pool"], w["pool_scale"], w["ffn_norm"],
      w["w_gate"], w["w_up"], w["w_down"], final_norm)


def _prep_layer(l, mix_norm, w_in, q_norm, kv_norm, w_uq, w_ukv, w_pool, pool_scale, sg_norm, w_s, b_s,
                w_o, ffn_norm, w_gate, w_up, w_down):
    o1 = Q_LORA_RANK
    o2 = o1 + KV_LORA_RANK
    o3 = o2 + QK_ROPE_DIM
    o4 = o3 + POOL_WIDTH
    half = QK_ROPE_DIM // 2

    def rot_cols(t):
        return jnp.concatenate([-t[..., half:], t[..., :half]], axis=-1)

    def pad_rope(t):
        z = jnp.zeros
        return jnp.concatenate([z((D_MODEL, ROPE_OFF), f32), t, z((D_MODEL, LANES - ROPE_OFF - QK_ROPE_DIM), f32)], axis=1)

    wi = w_in[l]
    w_kr = wi[:, o2:o3]
    w_in_pad = jnp.concatenate([wi[:, :o2], wi[:, o3:], pad_rope(w_kr), pad_rope(rot_cols(w_kr))], axis=1)

    wq = w_uq[l].reshape(Q_LORA_RANK, MLA_HEADS, QK_NOPE_DIM + QK_ROPE_DIM)
    wq_rope = wq[..., QK_NOPE_DIM:]
    wq_pad = jnp.concatenate([wq[..., :QK_NOPE_DIM], wq_rope, rot_cols(wq_rope)], axis=-1)
    w_uq_t = wq_pad.reshape(Q_LORA_RANK, MLA_HEADS * HEAD_PAD).T

    wkv = w_ukv[l].reshape(KV_LORA_RANK, MLA_HEADS, QK_NOPE_DIM + V_HEAD_DIM)
    w_k = jnp.concatenate([wkv[..., :QK_NOPE_DIM],
                           jnp.zeros((KV_LORA_RANK, MLA_HEADS, HEAD_PAD - QK_NOPE_DIM), f32)], axis=-1)
    w_k = w_k.reshape(KV_LORA_RANK, MLA_HEADS * HEAD_PAD)
    w_uv_t = wkv[..., QK_NOPE_DIM:].reshape(KV_LORA_RANK, MLA_WIDTH).T

    n_groups = len(POOL_WINDOWS)
    eye = jnp.eye(n_groups, dtype=f32)
    w_pool_bd = (eye[:, None, :, None] * w_pool[l][:, :, None, :]).reshape(POOL_WIDTH, POOL_WIDTH)

    sg_bias = jnp.repeat(b_s[l].T, SG_HEAD_DIM, axis=1)
    return dict(
        mix_norm=mix_norm[l].reshape(1, D_MODEL),
        w_in=w_in_pad.astype(bf16),
        q_norm=q_norm[l].reshape(1, Q_LORA_RANK),
        kv_norm=kv_norm[l].reshape(1, KV_LORA_RANK),
        w_uq_t=w_uq_t.astype(bf16),
        w_k=w_k.astype(bf16),
        w_uv_t=w_uv_t.astype(bf16),
        w_s=w_s[l].astype(bf16),
        sg_bias=sg_bias,
        sg_norm=sg_norm[l].reshape(1, SG_WIDTH),
        w_pool=w_pool_bd.astype(bf16),
        pool_scale=pool_scale[l].reshape(1, POOL_WIDTH),
        w_o=w_o[l].astype(bf16),
        ffn_norm=ffn_norm[l].reshape(1, D_MODEL),
        w_gate=w_gate[l].astype(bf16),
        w_up=w_up[l].astype(bf16),
        w_down=w_down[l].astype(bf16),
    )


def kernel(x, positions, mix_norm, w_in, q_norm, kv_norm, w_uq, w_ukv, w_pool, pool_scale, sg_norm, w_s, b_s,
           w_o, ffn_norm, w_gate, w_up, w_down, final_norm):
    batch, seq, _ = x.shape
    depth = w_in.shape[0]
    assert seq % TM_PROJ == 0 and seq % TM_POST == 0 and seq % TQ == 0 and seq % TK == 0
    tables = _rope_tables(positions)
    x2 = x.reshape(batch * seq, D_MODEL)
    fin = final_norm.reshape(1, D_MODEL)
    for l in range(depth):
        w = _prep_layer(l, mix_norm, w_in, q_norm, kv_norm, w_uq, w_ukv, w_pool, pool_scale, sg_norm, w_s, b_s,
                        w_o, ffn_norm, w_gate, w_up, w_down)
        q_t, k, v_t, hp, c = _proj_call(x2, w, tables, batch, seq)
        a = _attn_call(q_t, k, v_t)
        x2 = _post_call(x2, a.reshape(batch * seq, MLA_WIDTH), hp, c, w, fin, seq, last=(l == depth - 1))
    return x2.reshape(batch, seq, D_MODEL)
```

```python
import functools

import jax
import jax.numpy as jnp
from jax import lax
from jax.experimental import pallas as pl
from jax.experimental.pallas import tpu as pltpu

D_MODEL = 1024
MLA_HEADS = 8
QK_NOPE_DIM = 64
QK_ROPE_DIM = 32
V_HEAD_DIM = 64
Q_LORA_RANK = 384
KV_LORA_RANK = 256
ROPE_THETA = 10000.0
MLA_WIDTH = MLA_HEADS * V_HEAD_DIM
POOL_WINDOWS = (2, 4, 8, 16)
POOL_GROUP_DIM = 64
POOL_WIDTH = len(POOL_WINDOWS) * POOL_GROUP_DIM
SG_HEADS = 4
SG_HEAD_DIM = 64
SG_WIDTH = SG_HEADS * SG_HEAD_DIM
SG_CHUNK = 128
D_FF = 2816
EPS = 1e-6
LOG2_E = 1.4426950408889634

LANES = 128
HEAD_PAD = 128
ROPE_OFF = QK_NOPE_DIM
BF16_ROWS = 16
V_PAD = V_HEAD_DIM + BF16_ROWS
POOL_HALO = 8
FF_CHUNK = 256

C_Q = 0
C_KV = C_Q + Q_LORA_RANK
C_POOL = C_KV + KV_LORA_RANK
C_SG = C_POOL + POOL_WIDTH
C_ROPE = C_SG + 2 * SG_WIDTH
C_ROT = C_ROPE + LANES
IN_COLS_PAD = C_ROT + LANES

TM_PROJ = 512
TM_POST = 512
TQ = 512
TK = 256
Q_SUBTILES = 2
TM_ROPE = 2048

VMEM_LIMIT = 56 * 1024 * 1024

f32 = jnp.float32
bf16 = jnp.bfloat16


def _rms(x, g):
    return x * lax.rsqrt(jnp.mean(x * x, axis=-1, keepdims=True) + EPS) * g


def _dot(a, b):
    return jnp.dot(a, b, preferred_element_type=f32)


def _dot_nt(a, b):
    return lax.dot_general(a, b, (((1,), (1,)), ((), ())), preferred_element_type=f32)


def _rope_kernel(pos_row_ref, pos_col_ref, inv_col_ref, inv_row_ref, mask_row_ref,
                 cos_t_ref, sin_t_ref, cos_k_ref, sin_k_ref):
    ang_t = pos_row_ref[...].astype(f32) * inv_col_ref[...]
    cos_t_ref[...] = jnp.cos(ang_t)
    sin_t_ref[...] = jnp.sin(ang_t)
    ang_k = pos_col_ref[...].astype(f32) * inv_row_ref[...]
    cos_k_ref[...] = jnp.cos(ang_k) * mask_row_ref[...]
    sin_k_ref[...] = jnp.sin(ang_k) * mask_row_ref[...]


def _rope_tables(positions):
    n = positions.size
    inv_freq = ROPE_THETA ** (-jnp.arange(0, QK_ROPE_DIM, 2, dtype=f32) / QK_ROPE_DIM)
    inv2 = jnp.concatenate([inv_freq, inv_freq])
    zeros = jnp.zeros
    inv_row = jnp.concatenate([zeros((ROPE_OFF,), f32), inv2, zeros((HEAD_PAD - ROPE_OFF - QK_ROPE_DIM,), f32)])
    mask_row = jnp.concatenate([zeros((ROPE_OFF,), f32), jnp.ones((QK_ROPE_DIM,), f32),
                                zeros((HEAD_PAD - ROPE_OFF - QK_ROPE_DIM,), f32)])
    tm = TM_ROPE
    return pl.pallas_call(
        _rope_kernel,
        grid=(n // tm,),
        in_specs=[
            pl.BlockSpec((1, tm), lambda i: (0, i)),
            pl.BlockSpec((tm, 1), lambda i: (i, 0)),
            pl.BlockSpec((QK_ROPE_DIM, 1), lambda i: (0, 0)),
            pl.BlockSpec((1, HEAD_PAD), lambda i: (0, 0)),
            pl.BlockSpec((1, HEAD_PAD), lambda i: (0, 0)),
        ],
        out_specs=[
            pl.BlockSpec((QK_ROPE_DIM, tm), lambda i: (0, i)),
            pl.BlockSpec((QK_ROPE_DIM, tm), lambda i: (0, i)),
            pl.BlockSpec((tm, HEAD_PAD), lambda i: (i, 0)),
            pl.BlockSpec((tm, HEAD_PAD), lambda i: (i, 0)),
        ],
        out_shape=[
            jax.ShapeDtypeStruct((QK_ROPE_DIM, n), f32),
            jax.ShapeDtypeStruct((QK_ROPE_DIM, n), f32),
            jax.ShapeDtypeStruct((n, HEAD_PAD), f32),
            jax.ShapeDtypeStruct((n, HEAD_PAD), f32),
        ],
        compiler_params=pltpu.CompilerParams(dimension_semantics=("parallel",)),
        name="rope_tables",
    )(positions.reshape(1, n), positions.reshape(n, 1), inv2.reshape(QK_ROPE_DIM, 1),
      inv_row.reshape(1, HEAD_PAD), mask_row.reshape(1, HEAD_PAD))


def _proj_kernel(x_ref, g_ref, win_ref, qn_ref, kvn_ref, wuq_t_ref, wk_ref, wuv_t_ref,
                 ws_ref, bias_ref, sgn_ref, cos_t_ref, sin_t_ref, cos_k_ref, sin_k_ref,
                 q_t_ref, k_ref, v_t_ref, hp_ref, c_ref):
    tm = x_ref.shape[0]
    scale = (QK_NOPE_DIM + QK_ROPE_DIM) ** -0.5 * LOG2_E
    h = _rms(x_ref[...], g_ref[...]).astype(bf16)
    p = _dot(h, win_ref[...])

    cq = _rms(p[:, C_Q:C_Q + Q_LORA_RANK], qn_ref[...]).astype(bf16)
    q_t = _dot_nt(wuq_t_ref[...], cq)
    cos_t = cos_t_ref[...]
    sin_t = sin_t_ref[...]
    for hd in range(MLA_HEADS):
        b0 = hd * HEAD_PAD
        nope = q_t[b0:b0 + QK_NOPE_DIM]
        rope = q_t[b0 + ROPE_OFF:b0 + ROPE_OFF + QK_ROPE_DIM]
        rot = q_t[b0 + ROPE_OFF + QK_ROPE_DIM:b0 + HEAD_PAD]
        q_t_ref[b0:b0 + QK_NOPE_DIM, :] = (nope * scale).astype(bf16)
        q_t_ref[b0 + ROPE_OFF:b0 + ROPE_OFF + QK_ROPE_DIM, :] = (
            (rope * cos_t + rot * sin_t) * scale).astype(bf16)
        q_t_ref[b0 + ROPE_OFF + QK_ROPE_DIM:b0 + HEAD_PAD, :] = jnp.zeros(
            (HEAD_PAD - ROPE_OFF - QK_ROPE_DIM, tm), bf16)

    ckv = _rms(p[:, C_KV:C_KV + KV_LORA_RANK], kvn_ref[...]).astype(bf16)
    k_nope = _dot(ckv, wk_ref[...])
    k_rope = (p[:, C_ROPE:C_ROPE + LANES] * cos_k_ref[...]
              + p[:, C_ROT:C_ROT + LANES] * sin_k_ref[...])
    for hd in range(MLA_HEADS):
        b0 = hd * HEAD_PAD
        k_ref[:, b0:b0 + HEAD_PAD] = (k_nope[:, b0:b0 + HEAD_PAD] + k_rope).astype(bf16)
    v_t = _dot_nt(wuv_t_ref[...], ckv).astype(bf16)
    for hd in range(MLA_HEADS):
        b0 = hd * V_PAD
        v_t_ref[b0:b0 + V_HEAD_DIM, :] = v_t[hd * V_HEAD_DIM:(hd + 1) * V_HEAD_DIM]
        v_t_ref[b0 + V_HEAD_DIM:b0 + V_PAD, :] = jnp.ones((V_PAD - V_HEAD_DIM, tm), bf16)

    hp_ref[...] = p[:, C_POOL:C_POOL + POOL_WIDTH]

    z = jax.nn.gelu(p[:, C_SG:C_SG + 2 * SG_WIDTH])
    u = z[:, :SG_WIDTH]
    v = z[:, SG_WIDTH:]
    grp = lax.broadcasted_iota(jnp.int32, (1, SG_WIDTH), 1) // SG_HEAD_DIM
    vsq = v * v
    inv = jnp.zeros_like(v)
    for g in range(SG_HEADS):
        sel = grp == g
        ms = jnp.sum(jnp.where(sel, vsq, 0.0), axis=-1, keepdims=True) / SG_HEAD_DIM
        inv = jnp.where(sel, lax.rsqrt(ms + EPS), inv)
    vn = (v * inv * sgn_ref[...]).astype(bf16)
    bias = bias_ref[...]
    for c in range(tm // SG_CHUNK):
        r0 = c * SG_CHUNK
        vc = vn[r0:r0 + SG_CHUNK]
        mixed = bias
        for g in range(SG_HEADS):
            mixed = mixed + jnp.where(grp == g, _dot(ws_ref[g], vc), 0.0)
        c_ref[r0:r0 + SG_CHUNK, :] = (u[r0:r0 + SG_CHUNK] * mixed).astype(bf16)


def _const_spec(shape):
    nd = len(shape)
    return pl.BlockSpec(shape, lambda *_: (0,) * nd)


def _layer_spec(shape, l, **kwargs):
    nd = len(shape)
    return pl.BlockSpec((None,) + tuple(shape), lambda *_: (l,) + (0,) * nd, **kwargs)


def _proj_call(x2, w, l, tables, batch, seq):
    n = x2.shape[0]
    tm = TM_PROJ
    cos_t, sin_t, cos_k, sin_k = tables
    row = lambda cols: pl.BlockSpec((tm, cols), lambda i: (i, 0))
    col = lambda rows: pl.BlockSpec((rows, tm), lambda i: (0, i))
    tiles_per_seq = seq // tm
    q_spec = pl.BlockSpec((None, MLA_HEADS * HEAD_PAD, tm), lambda i: (i // tiles_per_seq, 0, i % tiles_per_seq))
    v_spec = pl.BlockSpec((None, MLA_HEADS * V_PAD, tm), lambda i: (i // tiles_per_seq, 0, i % tiles_per_seq))
    k_spec = pl.BlockSpec((None, tm, MLA_HEADS * HEAD_PAD), lambda i: (i // tiles_per_seq, i % tiles_per_seq, 0))
    return pl.pallas_call(
        _proj_kernel,
        grid=(n // tm,),
        in_specs=[
            row(D_MODEL),
            _layer_spec((1, D_MODEL), l),
            _layer_spec((D_MODEL, IN_COLS_PAD), l),
            _layer_spec((1, Q_LORA_RANK), l),
            _layer_spec((1, KV_LORA_RANK), l),
            _layer_spec((MLA_HEADS * HEAD_PAD, Q_LORA_RANK), l),
            _layer_spec((KV_LORA_RANK, MLA_HEADS * HEAD_PAD), l),
            _layer_spec((MLA_WIDTH, KV_LORA_RANK), l),
            _layer_spec((SG_HEADS, SG_CHUNK, SG_CHUNK), l),
            _layer_spec((SG_CHUNK, SG_WIDTH), l),
            _layer_spec((1, SG_WIDTH), l),
            col(QK_ROPE_DIM),
            col(QK_ROPE_DIM),
            row(HEAD_PAD),
            row(HEAD_PAD),
        ],
        out_specs=[q_spec, k_spec, v_spec, row(POOL_WIDTH), row(SG_WIDTH)],
        out_shape=[
            jax.ShapeDtypeStruct((batch, MLA_HEADS * HEAD_PAD, seq), bf16),
            jax.ShapeDtypeStruct((batch, seq, MLA_HEADS * HEAD_PAD), bf16),
            jax.ShapeDtypeStruct((batch, MLA_HEADS * V_PAD, seq), bf16),
            jax.ShapeDtypeStruct((n, POOL_WIDTH), f32),
            jax.ShapeDtypeStruct((n, SG_WIDTH), bf16),
        ],
        compiler_params=pltpu.CompilerParams(dimension_semantics=("parallel",), vmem_limit_bytes=VMEM_LIMIT),
        name="proj",
    )(x2, w["mix_norm"], w["w_in"], w["q_norm"], w["kv_norm"], w["w_uq_t"], w["w_k"], w["w_uv_t"],
      w["w_s"], w["sg_bias"], w["sg_norm"], cos_t, sin_t, cos_k, sin_k)


def _attn_kernel(q_t_ref, k_ref, v_t_ref, o_ref, s_buf, p_buf, acc_ref, *, tq, tk):
    seq = k_ref.shape[0]
    n_kv = seq // tk
    assert n_kv >= 3
    n_sub = q_t_ref.shape[1] // tq

    def stream(sid, qs, hh):
        q_t = q_t_ref[hh * HEAD_PAD:(hh + 1) * HEAD_PAD, qs * tq:(qs + 1) * tq]

        def stage_a(t):
            s_t = _dot(k_ref[t * tk:(t + 1) * tk, hh * HEAD_PAD:(hh + 1) * HEAD_PAD], q_t)
            s_buf[sid, t % 2] = s_t
            return jnp.max(s_t, axis=0, keepdims=True)

        def stage_b(t, cmax, m):
            m_new = cmax if m is None else jnp.maximum(m, cmax)
            p_buf[sid, t % 2] = jnp.exp2(s_buf[sid, t % 2] - m_new).astype(bf16)
            alpha = None if m is None else jnp.exp2(m - m_new)
            return m_new, alpha

        def stage_c(t, alpha):
            v_c = v_t_ref[hh * V_PAD:(hh + 1) * V_PAD, t * tk:(t + 1) * tk]
            part = _dot(v_c, p_buf[sid, t % 2])
            acc_ref[sid] = part if alpha is None else alpha * acc_ref[sid] + part

        cmax, m, alphas = None, None, {}
        for t in range(n_kv + 2):
            cmax_new = stage_a(t) if t < n_kv else None
            if 1 <= t <= n_kv:
                m, alphas[t - 1] = stage_b(t - 1, cmax, m)
            if t >= 2:
                stage_c(t - 2, alphas.pop(t - 2))
            cmax = cmax_new
        acc = acc_ref[sid]
        return acc[:V_HEAD_DIM] / acc[V_HEAD_DIM:V_HEAD_DIM + 1]

    for qs in range(n_sub):
        o_t = jnp.concatenate([stream(2 * qs + hh, qs, hh) for hh in range(2)], axis=0)
        o_ref[qs * tq:(qs + 1) * tq, :] = o_t.T.astype(o_ref.dtype)


def _attn_call(q_t, k, v_t):
    batch, _, seq = q_t.shape
    pairs = MLA_HEADS // 2
    n_streams = 2 * Q_SUBTILES
    tq_blk = TQ * Q_SUBTILES
    return pl.pallas_call(
        functools.partial(_attn_kernel, tq=TQ, tk=TK),
        grid=(batch, pairs, seq // tq_blk),
        in_specs=[
            pl.BlockSpec((None, 2 * HEAD_PAD, tq_blk), lambda b, h, q: (b, h, q)),
            pl.BlockSpec((None, seq, 2 * HEAD_PAD), lambda b, h, q: (b, 0, h)),
            pl.BlockSpec((None, 2 * V_PAD, seq), lambda b, h, q: (b, h, 0)),
        ],
        out_specs=pl.BlockSpec((None, tq_blk, 2 * V_HEAD_DIM), lambda b, h, q: (b, q, h)),
        out_shape=jax.ShapeDtypeStruct((batch, seq, MLA_WIDTH), bf16),
        scratch_shapes=[pltpu.VMEM((n_streams, 2, TK, TQ), f32), pltpu.VMEM((n_streams, 2, TK, TQ), bf16),
                        pltpu.VMEM((n_streams, V_PAD, TQ), f32)],
        compiler_params=pltpu.CompilerParams(
            dimension_semantics=("parallel", "parallel", "arbitrary"), vmem_limit_bytes=VMEM_LIMIT),
        name="attn",
    )(q_t, k, v_t)


def _post_kernel(x_ref, a_ref, hp_ref, hp_prev_ref, hp_next_ref, c_ref, wo_ref, wpool_ref, pscale_ref,
                 fn_ref, wg_ref, wu_ref, wd_ref, final_ref, o_ref, ext_ref, a1_ref, a2_ref, a4_ref, *, seq, last):
    tm = x_ref.shape[0]
    tiles_per_seq = seq // tm
    si = pl.program_id(0) % tiles_per_seq
    hp = hp_ref[...]
    assert POOL_WINDOWS == (2, 4, 8, 16) and POOL_WIDTH == 2 * LANES

    h8 = POOL_HALO
    t0 = 2 * h8
    rows = tm + 3 * h8
    zero_rows = jnp.zeros((h8, POOL_WIDTH), f32)
    ext_ref[0:h8, :] = zero_rows
    ext_ref[h8:t0, :] = jnp.where(si > 0, hp_prev_ref[...], 0.0)
    ext_ref[t0:t0 + tm, :] = hp
    ext_ref[t0 + tm:rows, :] = jnp.where(si < tiles_per_seq - 1, hp_next_ref[...], 0.0)
    a1_ref[0:h8, :] = zero_rows
    a2_ref[0:h8, :] = zero_rows[:, :LANES]
    a1_ref[h8:rows, :] = ext_ref[h8:rows, :] + ext_ref[h8 - 1:rows - 1, :]
    a2_ref[h8:rows, :] = a1_ref[h8:rows, LANES:] + a1_ref[h8 - 2:rows - 2, LANES:]
    a4_ref[h8:rows, :] = a2_ref[h8:rows, :] + a2_ref[h8 - 4:rows - 4, :]

    def window(ref, lanes, half_w):
        end = t0 + half_w - 1
        return ref[end:end + tm, lanes] + ref[end - half_w:end - half_w + tm, lanes]

    lo = slice(0, LANES)
    sums = [a1_ref[t0:t0 + tm, lo], window(a1_ref, lo, 2), window(a2_ref, slice(None), 4),
            window(a4_ref, slice(None), 8)]
    t_abs = si * tm + lax.broadcasted_iota(jnp.int32, (tm, 1), 0)
    counts = []
    for w in POOL_WINDOWS:
        left = w // 2
        right = w - 1 - left
        counts.append((jnp.minimum(t_abs + right + 1, seq) - jnp.maximum(t_abs - left, 0)).astype(f32))
    first = lax.broadcasted_iota(jnp.int32, (1, LANES), 1) < POOL_GROUP_DIM
    mean = jnp.concatenate(
        [jnp.where(first, sums[0], sums[1]) / jnp.where(first, counts[0], counts[1]),
         jnp.where(first, sums[2], sums[3]) / jnp.where(first, counts[2], counts[3])], axis=1)
    d = (mean - hp).astype(bf16)
    b = _dot(d, wpool_ref[...]) * pscale_ref[...]

    mixo = (_dot(a_ref[...], wo_ref[0:MLA_WIDTH, :])
            + _dot(b.astype(bf16), wo_ref[MLA_WIDTH:MLA_WIDTH + POOL_WIDTH, :])
            + _dot(c_ref[...], wo_ref[MLA_WIDTH + POOL_WIDTH:, :]))
    x1 = x_ref[...] + mixo

    hn = _rms(x1, fn_ref[...]).astype(bf16)
    y = None
    for f in range(D_FF // FF_CHUNK):
        f0 = f * FF_CHUNK
        gte = _dot(hn, wg_ref[:, f0:f0 + FF_CHUNK])
        up = _dot(hn, wu_ref[:, f0:f0 + FF_CHUNK])
        act = (jax.nn.silu(gte) * up).astype(bf16)
        part = _dot(act, wd_ref[f0:f0 + FF_CHUNK, :])
        y = part if y is None else y + part
    out = x1 + y
    if last:
        out = _rms(out, final_ref[...])
    o_ref[...] = out


def _post_call(x2, a2, hp, c, w, l, final_norm, seq, last):
    n = x2.shape[0]
    tm = TM_POST
    hb = tm // POOL_HALO
    n_halo_blocks = n // POOL_HALO
    ext_rows = tm + 3 * POOL_HALO
    row = lambda cols: pl.BlockSpec((tm, cols), lambda i: (i, 0))
    resident = lambda shape: _layer_spec(shape, l, pipeline_mode=pl.Buffered(1))
    return pl.pallas_call(
        functools.partial(_post_kernel, seq=seq, last=last),
        grid=(n // tm,),
        in_specs=[
            row(D_MODEL),
            row(MLA_WIDTH),
            row(POOL_WIDTH),
            pl.BlockSpec((POOL_HALO, POOL_WIDTH), lambda i: (jnp.maximum(i * hb - 1, 0), 0)),
            pl.BlockSpec((POOL_HALO, POOL_WIDTH), lambda i: (jnp.minimum((i + 1) * hb, n_halo_blocks - 1), 0)),
            row(SG_WIDTH),
            resident((D_MODEL, D_MODEL)),
            resident((POOL_WIDTH, POOL_WIDTH)),
            _layer_spec((1, POOL_WIDTH), l),
            _layer_spec((1, D_MODEL), l),
            resident((D_MODEL, D_FF)),
            resident((D_MODEL, D_FF)),
            resident((D_FF, D_MODEL)),
            _const_spec((1, D_MODEL)),
        ],
        out_specs=row(D_MODEL),
        out_shape=jax.ShapeDtypeStruct((n, D_MODEL), f32),
        scratch_shapes=[pltpu.VMEM((ext_rows, POOL_WIDTH), f32), pltpu.VMEM((ext_rows, POOL_WIDTH), f32),
                        pltpu.VMEM((ext_rows, LANES), f32), pltpu.VMEM((ext_rows, LANES), f32)],
        compiler_params=pltpu.CompilerParams(dimension_semantics=("parallel",), vmem_limit_bytes=VMEM_LIMIT),
        name="post_last" if last else "post",
    )(x2, a2, hp, hp, hp, c, w["w_o"], w["w_pool"], w["pool_scale"], w["ffn_norm"],
      w["w_gate"], w["w_up"], w["w_down"], final_norm)


def _prep_weights(mix_norm, w_in, q_norm, kv_norm, w_uq, w_ukv, w_pool, pool_scale, sg_norm, w_s, b_s,
                  w_o, ffn_norm, w_gate, w_up, w_down):
    depth = w_in.shape[0]
    o1 = Q_LORA_RANK
    o2 = o1 + KV_LORA_RANK
    o3 = o2 + QK_ROPE_DIM
    half = QK_ROPE_DIM // 2
    zeros = jnp.zeros

    def rot_cols(t):
        return jnp.concatenate([-t[..., half:], t[..., :half]], axis=-1)

    def pad_rope(t):
        return jnp.concatenate([zeros((depth, D_MODEL, ROPE_OFF), f32), t,
                                zeros((depth, D_MODEL, LANES - ROPE_OFF - QK_ROPE_DIM), f32)], axis=-1)

    w_kr = w_in[:, :, o2:o3]
    w_in_pad = jnp.concatenate([w_in[:, :, :o2], w_in[:, :, o3:], pad_rope(w_kr), pad_rope(rot_cols(w_kr))], axis=-1)

    wq = w_uq.reshape(depth, Q_LORA_RANK, MLA_HEADS, QK_NOPE_DIM + QK_ROPE_DIM)
    wq_rope = wq[..., QK_NOPE_DIM:]
    wq_pad = jnp.concatenate([wq[..., :QK_NOPE_DIM], wq_rope, rot_cols(wq_rope)], axis=-1)
    w_uq_t = jnp.swapaxes(wq_pad.reshape(depth, Q_LORA_RANK, MLA_HEADS * HEAD_PAD), 1, 2)

    wkv = w_ukv.reshape(depth, KV_LORA_RANK, MLA_HEADS, QK_NOPE_DIM + V_HEAD_DIM)
    w_k = jnp.concatenate([wkv[..., :QK_NOPE_DIM],
                           zeros((depth, KV_LORA_RANK, MLA_HEADS, HEAD_PAD - QK_NOPE_DIM), f32)], axis=-1)
    w_k = w_k.reshape(depth, KV_LORA_RANK, MLA_HEADS * HEAD_PAD)
    w_uv_t = jnp.swapaxes(wkv[..., QK_NOPE_DIM:].reshape(depth, KV_LORA_RANK, MLA_WIDTH), 1, 2)

    n_groups = len(POOL_WINDOWS)
    eye = jnp.eye(n_groups, dtype=f32)
    w_pool_bd = (eye[None, :, None, :, None] * w_pool[:, :, :, None, :]).reshape(depth, POOL_WIDTH, POOL_WIDTH)

    sg_bias = jnp.repeat(jnp.swapaxes(b_s, 1, 2), SG_HEAD_DIM, axis=2)
    return dict(
        mix_norm=mix_norm.reshape(depth, 1, D_MODEL),
        w_in=w_in_pad.astype(bf16),
        q_norm=q_norm.reshape(depth, 1, Q_LORA_RANK),
        kv_norm=kv_norm.reshape(depth, 1, KV_LORA_RANK),
        w_uq_t=w_uq_t.astype(bf16),
        w_k=w_k.astype(bf16),
        w_uv_t=w_uv_t.astype(bf16),
        w_s=w_s.astype(bf16),
        sg_bias=sg_bias,
        sg_norm=sg_norm.reshape(depth, 1, SG_WIDTH),
        w_pool=w_pool_bd.astype(bf16),
        pool_scale=pool_scale.reshape(depth, 1, POOL_WIDTH),
        w_o=w_o.astype(bf16),
        ffn_norm=ffn_norm.reshape(depth, 1, D_MODEL),
        w_gate=w_gate.astype(bf16),
        w_up=w_up.astype(bf16),
        w_down=w_down.astype(bf16),
    )


def kernel(x, positions, mix_norm, w_in, q_norm, kv_norm, w_uq, w_ukv, w_pool, pool_scale, sg_norm, w_s, b_s,
           w_o, ffn_norm, w_gate, w_up, w_down, final_norm):
    batch, seq, _ = x.shape
    depth = w_in.shape[0]
    assert seq % TM_PROJ == 0 and seq % TM_POST == 0 and seq % (TQ * Q_SUBTILES) == 0 and seq % TK == 0
    tables = _rope_tables(positions)
    x2 = x.reshape(batch * seq, D_MODEL)
    fin = final_norm.reshape(1, D_MODEL)
    w = _prep_weights(mix_norm, w_in, q_norm, kv_norm, w_uq, w_ukv, w_pool, pool_scale, sg_norm, w_s, b_s,
                      w_o, ffn_norm, w_gate, w_up, w_down)
    for l in range(depth):
        q_t, k, v_t, hp, c = _proj_call(x2, w, l, tables, batch, seq)
        a = _attn_call(q_t, k, v_t)
        x2 = _post_call(x2, a.reshape(batch * seq, MLA_WIDTH), hp, c, w, l, fin, seq, last=(l == depth - 1))
    return x2.reshape(batch, seq, D_MODEL)
```

```python
import functools

import jax
import jax.numpy as jnp
from jax import lax
from jax.experimental import pallas as pl
from jax.experimental.pallas import tpu as pltpu

D_MODEL = 1024
MLA_HEADS = 8
QK_NOPE_DIM = 64
QK_ROPE_DIM = 32
V_HEAD_DIM = 64
Q_LORA_RANK = 384
KV_LORA_RANK = 256
ROPE_THETA = 10000.0
MLA_WIDTH = MLA_HEADS * V_HEAD_DIM
POOL_WINDOWS = (2, 4, 8, 16)
POOL_GROUP_DIM = 64
POOL_WIDTH = len(POOL_WINDOWS) * POOL_GROUP_DIM
SG_HEADS = 4
SG_HEAD_DIM = 64
SG_WIDTH = SG_HEADS * SG_HEAD_DIM
SG_CHUNK = 128
D_FF = 2816
EPS = 1e-6
LOG2_E = 1.4426950408889634

LANES = 128
HEAD_PAD = 128
ROPE_OFF = QK_NOPE_DIM
BF16_ROWS = 16
V_PAD = V_HEAD_DIM + BF16_ROWS
POOL_HALO = 8
FF_CHUNK = 256

C_Q = 0
C_KV = C_Q + Q_LORA_RANK
C_POOL = C_KV + KV_LORA_RANK
C_SG = C_POOL + POOL_WIDTH
C_ROPE = C_SG + 2 * SG_WIDTH
C_ROT = C_ROPE + LANES
IN_COLS_PAD = C_ROT + LANES

TM_PROJ = 512
TM_POST = 512
TQ = 512
TK = 256
Q_SUBTILES = 2
TM_ROPE = 2048

VMEM_LIMIT = 56 * 1024 * 1024

f32 = jnp.float32
bf16 = jnp.bfloat16


def _rms(x, g):
    return x * lax.rsqrt(jnp.mean(x * x, axis=-1, keepdims=True) + EPS) * g


def _dot(a, b):
    return jnp.dot(a, b, preferred_element_type=f32)


def _dot_nt(a, b):
    return lax.dot_general(a, b, (((1,), (1,)), ((), ())), preferred_element_type=f32)


def _rope_kernel(pos_ref, inv_ref, cos_t_ref, sin_t_ref, cos_k_ref, sin_k_ref):
    tm = pos_ref.shape[1]
    ang_t = pos_ref[...].astype(f32) * inv_ref[...]
    cos_t = jnp.cos(ang_t)
    sin_t = jnp.sin(ang_t)
    cos_t_ref[...] = cos_t
    sin_t_ref[...] = sin_t
    above = jnp.zeros((ROPE_OFF, tm), f32)
    below = jnp.zeros((HEAD_PAD - ROPE_OFF - QK_ROPE_DIM, tm), f32)
    cos_k_ref[...] = jnp.concatenate([above, cos_t, below], axis=0).T
    sin_k_ref[...] = jnp.concatenate([above, sin_t, below], axis=0).T


def _rope_tables(positions):
    n = positions.size
    inv_freq = ROPE_THETA ** (-jnp.arange(0, QK_ROPE_DIM, 2, dtype=f32) / QK_ROPE_DIM)
    inv2 = jnp.concatenate([inv_freq, inv_freq])
    tm = TM_ROPE
    return pl.pallas_call(
        _rope_kernel,
        grid=(n // tm,),
        in_specs=[
            pl.BlockSpec((1, tm), lambda i: (0, i)),
            pl.BlockSpec((QK_ROPE_DIM, 1), lambda i: (0, 0)),
        ],
        out_specs=[
            pl.BlockSpec((QK_ROPE_DIM, tm), lambda i: (0, i)),
            pl.BlockSpec((QK_ROPE_DIM, tm), lambda i: (0, i)),
            pl.BlockSpec((tm, HEAD_PAD), lambda i: (i, 0)),
            pl.BlockSpec((tm, HEAD_PAD), lambda i: (i, 0)),
        ],
        out_shape=[
            jax.ShapeDtypeStruct((QK_ROPE_DIM, n), f32),
            jax.ShapeDtypeStruct((QK_ROPE_DIM, n), f32),
            jax.ShapeDtypeStruct((n, HEAD_PAD), f32),
            jax.ShapeDtypeStruct((n, HEAD_PAD), f32),
        ],
        compiler_params=pltpu.CompilerParams(dimension_semantics=("parallel",)),
        name="rope_tables",
    )(positions.reshape(1, n), inv2.reshape(QK_ROPE_DIM, 1))


def _proj_kernel(x_ref, g_ref, win_ref, qn_ref, kvn_ref, wuq_t_ref, wk_ref, wuv_t_ref,
                 ws_ref, bias_ref, sgn_ref, cos_t_ref, sin_t_ref, cos_k_ref, sin_k_ref,
                 q_t_ref, k_ref, v_t_ref, hp_ref, c_ref):
    tm = x_ref.shape[0]
    scale = (QK_NOPE_DIM + QK_ROPE_DIM) ** -0.5 * LOG2_E
    h = _rms(x_ref[...], g_ref[...]).astype(bf16)
    p = _dot(h, win_ref[...])

    cq = _rms(p[:, C_Q:C_Q + Q_LORA_RANK], qn_ref[...]).astype(bf16)
    q_t = _dot_nt(wuq_t_ref[...], cq)
    cos_t = cos_t_ref[...]
    sin_t = sin_t_ref[...]
    for hd in range(MLA_HEADS):
        b0 = hd * HEAD_PAD
        nope = q_t[b0:b0 + QK_NOPE_DIM]
        rope = q_t[b0 + ROPE_OFF:b0 + ROPE_OFF + QK_ROPE_DIM]
        rot = q_t[b0 + ROPE_OFF + QK_ROPE_DIM:b0 + HEAD_PAD]
        q_t_ref[b0:b0 + QK_NOPE_DIM, :] = (nope * scale).astype(bf16)
        q_t_ref[b0 + ROPE_OFF:b0 + ROPE_OFF + QK_ROPE_DIM, :] = (
            (rope * cos_t + rot * sin_t) * scale).astype(bf16)
        q_t_ref[b0 + ROPE_OFF + QK_ROPE_DIM:b0 + HEAD_PAD, :] = jnp.zeros(
            (HEAD_PAD - ROPE_OFF - QK_ROPE_DIM, tm), bf16)

    ckv = _rms(p[:, C_KV:C_KV + KV_LORA_RANK], kvn_ref[...]).astype(bf16)
    k_nope = _dot(ckv, wk_ref[...])
    k_rope = (p[:, C_ROPE:C_ROPE + LANES] * cos_k_ref[...]
              + p[:, C_ROT:C_ROT + LANES] * sin_k_ref[...])
    for hd in range(MLA_HEADS):
        b0 = hd * HEAD_PAD
        k_ref[:, b0:b0 + HEAD_PAD] = (k_nope[:, b0:b0 + HEAD_PAD] + k_rope).astype(bf16)
    v_t = _dot_nt(wuv_t_ref[...], ckv).astype(bf16)
    for hd in range(MLA_HEADS):
        b0 = hd * V_PAD
        v_t_ref[b0:b0 + V_HEAD_DIM, :] = v_t[hd * V_HEAD_DIM:(hd + 1) * V_HEAD_DIM]
        v_t_ref[b0 + V_HEAD_DIM:b0 + V_PAD, :] = jnp.ones((V_PAD - V_HEAD_DIM, tm), bf16)

    hp_ref[...] = p[:, C_POOL:C_POOL + POOL_WIDTH]

    z = jax.nn.gelu(p[:, C_SG:C_SG + 2 * SG_WIDTH])
    u = z[:, :SG_WIDTH]
    v = z[:, SG_WIDTH:]
    grp = lax.broadcasted_iota(jnp.int32, (1, SG_WIDTH), 1) // SG_HEAD_DIM
    vsq = v * v
    inv = jnp.zeros_like(v)
    for g in range(SG_HEADS):
        sel = grp == g
        ms = jnp.sum(jnp.where(sel, vsq, 0.0), axis=-1, keepdims=True) / SG_HEAD_DIM
        inv = jnp.where(sel, lax.rsqrt(ms + EPS), inv)
    vn = (v * inv * sgn_ref[...]).astype(bf16)
    bias = bias_ref[...]
    ws_cat = ws_ref[...]
    zero = jnp.zeros((), bf16)
    for c in range(tm // SG_CHUNK):
        r0 = c * SG_CHUNK
        vc = vn[r0:r0 + SG_CHUNK]
        v_blocks = jnp.concatenate([jnp.where(grp == g, vc, zero) for g in range(SG_HEADS)], axis=0)
        mixed = _dot(ws_cat, v_blocks) + bias
        c_ref[r0:r0 + SG_CHUNK, :] = (u[r0:r0 + SG_CHUNK] * mixed).astype(bf16)


def _const_spec(shape):
    nd = len(shape)
    return pl.BlockSpec(shape, lambda *_: (0,) * nd)


def _layer_spec(shape, l, **kwargs):
    nd = len(shape)
    return pl.BlockSpec((None,) + tuple(shape), lambda *_: (l,) + (0,) * nd, **kwargs)


def _proj_call(x2, w, l, tables, batch, seq):
    n = x2.shape[0]
    tm = TM_PROJ
    cos_t, sin_t, cos_k, sin_k = tables
    row = lambda cols: pl.BlockSpec((tm, cols), lambda i: (i, 0))
    col = lambda rows: pl.BlockSpec((rows, tm), lambda i: (0, i))
    tiles_per_seq = seq // tm
    q_spec = pl.BlockSpec((None, MLA_HEADS * HEAD_PAD, tm), lambda i: (i // tiles_per_seq, 0, i % tiles_per_seq))
    v_spec = pl.BlockSpec((None, MLA_HEADS * V_PAD, tm), lambda i: (i // tiles_per_seq, 0, i % tiles_per_seq))
    k_spec = pl.BlockSpec((None, tm, MLA_HEADS * HEAD_PAD), lambda i: (i // tiles_per_seq, i % tiles_per_seq, 0))
    return pl.pallas_call(
        _proj_kernel,
        grid=(n // tm,),
        in_specs=[
            row(D_MODEL),
            _layer_spec((1, D_MODEL), l),
            _layer_spec((D_MODEL, IN_COLS_PAD), l),
            _layer_spec((1, Q_LORA_RANK), l),
            _layer_spec((1, KV_LORA_RANK), l),
            _layer_spec((MLA_HEADS * HEAD_PAD, Q_LORA_RANK), l),
            _layer_spec((KV_LORA_RANK, MLA_HEADS * HEAD_PAD), l),
            _layer_spec((MLA_WIDTH, KV_LORA_RANK), l),
            _layer_spec((SG_CHUNK, SG_HEADS * SG_CHUNK), l),
            _layer_spec((SG_CHUNK, SG_WIDTH), l),
            _layer_spec((1, SG_WIDTH), l),
            col(QK_ROPE_DIM),
            col(QK_ROPE_DIM),
            row(HEAD_PAD),
            row(HEAD_PAD),
        ],
        out_specs=[q_spec, k_spec, v_spec, row(POOL_WIDTH), row(SG_WIDTH)],
        out_shape=[
            jax.ShapeDtypeStruct((batch, MLA_HEADS * HEAD_PAD, seq), bf16),
            jax.ShapeDtypeStruct((batch, seq, MLA_HEADS * HEAD_PAD), bf16),
            jax.ShapeDtypeStruct((batch, MLA_HEADS * V_PAD, seq), bf16),
            jax.ShapeDtypeStruct((n, POOL_WIDTH), f32),
            jax.ShapeDtypeStruct((n, SG_WIDTH), bf16),
        ],
        compiler_params=pltpu.CompilerParams(dimension_semantics=("parallel",), vmem_limit_bytes=VMEM_LIMIT),
        name="proj",
    )(x2, w["mix_norm"], w["w_in"], w["q_norm"], w["kv_norm"], w["w_uq_t"], w["w_k"], w["w_uv_t"],
      w["w_s"], w["sg_bias"], w["sg_norm"], cos_t, sin_t, cos_k, sin_k)


def _attn_kernel(q_t_ref, k_ref, v_t_ref, o_ref, s_buf, p_buf, acc_ref, *, tq, tk):
    seq = k_ref.shape[0]
    n_kv = seq // tk
    assert n_kv >= 3
    n_sub = q_t_ref.shape[1] // tq

    def stream(sid, qs, hh):
        q_t = q_t_ref[hh * HEAD_PAD:(hh + 1) * HEAD_PAD, qs * tq:(qs + 1) * tq]

        def stage_a(t):
            s_t = _dot(k_ref[t * tk:(t + 1) * tk, hh * HEAD_PAD:(hh + 1) * HEAD_PAD], q_t)
            s_buf[sid, t % 2] = s_t
            return jnp.max(s_t, axis=0, keepdims=True)

        def stage_b(t, cmax, m):
            m_new = cmax if m is None else jnp.maximum(m, cmax)
            p_buf[sid, t % 2] = jnp.exp2(s_buf[sid, t % 2] - m_new).astype(bf16)
            alpha = None if m is None else jnp.exp2(m - m_new)
            return m_new, alpha

        def stage_c(t, alpha):
            v_c = v_t_ref[hh * V_PAD:(hh + 1) * V_PAD, t * tk:(t + 1) * tk]
            part = _dot(v_c, p_buf[sid, t % 2])
            acc_ref[sid] = part if alpha is None else alpha * acc_ref[sid] + part

        cmax, m, alphas = None, None, {}
        for t in range(n_kv + 2):
            cmax_new = stage_a(t) if t < n_kv else None
            if 1 <= t <= n_kv:
                m, alphas[t - 1] = stage_b(t - 1, cmax, m)
            if t >= 2:
                stage_c(t - 2, alphas.pop(t - 2))
            cmax = cmax_new
        acc = acc_ref[sid]
        return acc[:V_HEAD_DIM] / acc[V_HEAD_DIM:V_HEAD_DIM + 1]

    for qs in range(n_sub):
        o_t = jnp.concatenate([stream(2 * qs + hh, qs, hh) for hh in range(2)], axis=0)
        o_ref[qs * tq:(qs + 1) * tq, :] = o_t.T.astype(o_ref.dtype)


def _attn_call(q_t, k, v_t):
    batch, _, seq = q_t.shape
    pairs = MLA_HEADS // 2
    n_streams = 2 * Q_SUBTILES
    tq_blk = TQ * Q_SUBTILES
    return pl.pallas_call(
        functools.partial(_attn_kernel, tq=TQ, tk=TK),
        grid=(batch, pairs, seq // tq_blk),
        in_specs=[
            pl.BlockSpec((None, 2 * HEAD_PAD, tq_blk), lambda b, h, q: (b, h, q)),
            pl.BlockSpec((None, seq, 2 * HEAD_PAD), lambda b, h, q: (b, 0, h)),
            pl.BlockSpec((None, 2 * V_PAD, seq), lambda b, h, q: (b, h, 0)),
        ],
        out_specs=pl.BlockSpec((None, tq_blk, 2 * V_HEAD_DIM), lambda b, h, q: (b, q, h)),
        out_shape=jax.ShapeDtypeStruct((batch, seq, MLA_WIDTH), bf16),
        scratch_shapes=[pltpu.VMEM((n_streams, 2, TK, TQ), f32), pltpu.VMEM((n_streams, 2, TK, TQ), bf16),
                        pltpu.VMEM((n_streams, V_PAD, TQ), f32)],
        compiler_params=pltpu.CompilerParams(
            dimension_semantics=("parallel", "parallel", "arbitrary"), vmem_limit_bytes=VMEM_LIMIT),
        name="attn",
    )(q_t, k, v_t)


def _post_kernel(x_ref, a_ref, hp_ref, hp_prev_ref, hp_next_ref, c_ref, wo_ref, wpool_ref, pscale_ref,
                 fn_ref, wg_ref, wu_ref, wd_ref, final_ref, o_ref, ext_ref, a1_ref, a2_ref, a4_ref, *, seq, last):
    tm = x_ref.shape[0]
    tiles_per_seq = seq // tm
    si = pl.program_id(0) % tiles_per_seq
    hp = hp_ref[...]
    assert POOL_WINDOWS == (2, 4, 8, 16) and POOL_WIDTH == 2 * LANES

    h8 = POOL_HALO
    t0 = 2 * h8
    rows = tm + 3 * h8
    zero_rows = jnp.zeros((h8, POOL_WIDTH), f32)
    ext_ref[0:h8, :] = zero_rows
    ext_ref[h8:t0, :] = jnp.where(si > 0, hp_prev_ref[...], 0.0)
    ext_ref[t0:t0 + tm, :] = hp
    ext_ref[t0 + tm:rows, :] = jnp.where(si < tiles_per_seq - 1, hp_next_ref[...], 0.0)
    a1_ref[0:h8, :] = zero_rows
    a2_ref[0:h8, :] = zero_rows[:, :LANES]
    a1_ref[h8:rows, :] = ext_ref[h8:rows, :] + ext_ref[h8 - 1:rows - 1, :]
    a2_ref[h8:rows, :] = a1_ref[h8:rows, LANES:] + a1_ref[h8 - 2:rows - 2, LANES:]
    a4_ref[h8:rows, :] = a2_ref[h8:rows, :] + a2_ref[h8 - 4:rows - 4, :]

    def window(ref, lanes, half_w):
        end = t0 + half_w - 1
        return ref[end:end + tm, lanes] + ref[end - half_w:end - half_w + tm, lanes]

    lo = slice(0, LANES)
    sums = [a1_ref[t0:t0 + tm, lo], window(a1_ref, lo, 2), window(a2_ref, slice(None), 4),
            window(a4_ref, slice(None), 8)]
    t_abs = si * tm + lax.broadcasted_iota(jnp.int32, (tm, 1), 0)
    counts = []
    for w in POOL_WINDOWS:
        left = w // 2
        right = w - 1 - left
        counts.append((jnp.minimum(t_abs + right + 1, seq) - jnp.maximum(t_abs - left, 0)).astype(f32))
    first = lax.broadcasted_iota(jnp.int32, (1, LANES), 1) < POOL_GROUP_DIM
    mean = jnp.concatenate(
        [jnp.where(first, sums[0], sums[1]) / jnp.where(first, counts[0], counts[1]),
         jnp.where(first, sums[2], sums[3]) / jnp.where(first, counts[2], counts[3])], axis=1)
    d = (mean - hp).astype(bf16)
    b = _dot(d, wpool_ref[...]) * pscale_ref[...]

    mixo = (_dot(a_ref[...], wo_ref[0:MLA_WIDTH, :])
            + _dot(b.astype(bf16), wo_ref[MLA_WIDTH:MLA_WIDTH + POOL_WIDTH, :])
            + _dot(c_ref[...], wo_ref[MLA_WIDTH + POOL_WIDTH:, :]))
    x1 = x_ref[...] + mixo

    hn = _rms(x1, fn_ref[...]).astype(bf16)
    y = None
    for f in range(D_FF // FF_CHUNK):
        f0 = f * FF_CHUNK
        gte = _dot(hn, wg_ref[:, f0:f0 + FF_CHUNK])
        up = _dot(hn, wu_ref[:, f0:f0 + FF_CHUNK])
        act = (jax.nn.silu(gte) * up).astype(bf16)
        part = _dot(act, wd_ref[f0:f0 + FF_CHUNK, :])
        y = part if y is None else y + part
    out = x1 + y
    if last:
        out = _rms(out, final_ref[...])
    o_ref[...] = out


def _post_call(x2, a2, hp, c, w, l, final_norm, seq, last):
    n = x2.shape[0]
    tm = TM_POST
    hb = tm // POOL_HALO
    n_halo_blocks = n // POOL_HALO
    ext_rows = tm + 3 * POOL_HALO
    row = lambda cols: pl.BlockSpec((tm, cols), lambda i: (i, 0))
    resident = lambda shape: _layer_spec(shape, l, pipeline_mode=pl.Buffered(1))
    return pl.pallas_call(
        functools.partial(_post_kernel, seq=seq, last=last),
        grid=(n // tm,),
        in_specs=[
            row(D_MODEL),
            row(MLA_WIDTH),
            row(POOL_WIDTH),
            pl.BlockSpec((POOL_HALO, POOL_WIDTH), lambda i: (jnp.maximum(i * hb - 1, 0), 0)),
            pl.BlockSpec((POOL_HALO, POOL_WIDTH), lambda i: (jnp.minimum((i + 1) * hb, n_halo_blocks - 1), 0)),
            row(SG_WIDTH),
            resident((D_MODEL, D_MODEL)),
            resident((POOL_WIDTH, POOL_WIDTH)),
            _layer_spec((1, POOL_WIDTH), l),
            _layer_spec((1, D_MODEL), l),
            resident((D_MODEL, D_FF)),
            resident((D_MODEL, D_FF)),
            resident((D_FF, D_MODEL)),
            _const_spec((1, D_MODEL)),
        ],
        out_specs=row(D_MODEL),
        out_shape=jax.ShapeDtypeStruct((n, D_MODEL), f32),
        scratch_shapes=[pltpu.VMEM((ext_rows, POOL_WIDTH), f32), pltpu.VMEM((ext_rows, POOL_WIDTH), f32),
                        pltpu.VMEM((ext_rows, LANES), f32), pltpu.VMEM((ext_rows, LANES), f32)],
        compiler_params=pltpu.CompilerParams(dimension_semantics=("parallel",), vmem_limit_bytes=VMEM_LIMIT),
        name="post_last" if last else "post",
    )(x2, a2, hp, hp, hp, c, w["w_o"], w["w_pool"], w["pool_scale"], w["ffn_norm"],
      w["w_gate"], w["w_up"], w["w_down"], final_norm)


def _prep_weights(mix_norm, w_in, q_norm, kv_norm, w_uq, w_ukv, w_pool, pool_scale, sg_norm, w_s, b_s,
                  w_o, ffn_norm, w_gate, w_up, w_down):
    depth = w_in.shape[0]
    o1 = Q_LORA_RANK
    o2 = o1 + KV_LORA_RANK
    o3 = o2 + QK_ROPE_DIM
    half = QK_ROPE_DIM // 2
    zeros = jnp.zeros

    def rot_cols(t):
        return jnp.concatenate([-t[..., half:], t[..., :half]], axis=-1)

    def pad_rope(t):
        return jnp.concatenate([zeros((depth, D_MODEL, ROPE_OFF), f32), t,
                                zeros((depth, D_MODEL, LANES - ROPE_OFF - QK_ROPE_DIM), f32)], axis=-1)

    w_kr = w_in[:, :, o2:o3]
    w_in_pad = jnp.concatenate([w_in[:, :, :o2], w_in[:, :, o3:], pad_rope(w_kr), pad_rope(rot_cols(w_kr))], axis=-1)

    wq = w_uq.reshape(depth, Q_LORA_RANK, MLA_HEADS, QK_NOPE_DIM + QK_ROPE_DIM)
    wq_rope = wq[..., QK_NOPE_DIM:]
    wq_pad = jnp.concatenate([wq[..., :QK_NOPE_DIM], wq_rope, rot_cols(wq_rope)], axis=-1)
    w_uq_t = jnp.swapaxes(wq_pad.reshape(depth, Q_LORA_RANK, MLA_HEADS * HEAD_PAD), 1, 2)

    wkv = w_ukv.reshape(depth, KV_LORA_RANK, MLA_HEADS, QK_NOPE_DIM + V_HEAD_DIM)
    w_k = jnp.concatenate([wkv[..., :QK_NOPE_DIM],
                           zeros((depth, KV_LORA_RANK, MLA_HEADS, HEAD_PAD - QK_NOPE_DIM), f32)], axis=-1)
    w_k = w_k.reshape(depth, KV_LORA_RANK, MLA_HEADS * HEAD_PAD)
    w_uv_t = jnp.swapaxes(wkv[..., QK_NOPE_DIM:].reshape(depth, KV_LORA_RANK, MLA_WIDTH), 1, 2)

    n_groups = len(POOL_WINDOWS)
    eye = jnp.eye(n_groups, dtype=f32)
    w_pool_bd = (eye[None, :, None, :, None] * w_pool[:, :, :, None, :]).reshape(depth, POOL_WIDTH, POOL_WIDTH)

    sg_bias = jnp.repeat(jnp.swapaxes(b_s, 1, 2), SG_HEAD_DIM, axis=2)
    return dict(
        mix_norm=mix_norm.reshape(depth, 1, D_MODEL),
        w_in=w_in_pad.astype(bf16),
        q_norm=q_norm.reshape(depth, 1, Q_LORA_RANK),
        kv_norm=kv_norm.reshape(depth, 1, KV_LORA_RANK),
        w_uq_t=w_uq_t.astype(bf16),
        w_k=w_k.astype(bf16),
        w_uv_t=w_uv_t.astype(bf16),
        w_s=jnp.swapaxes(w_s, 1, 2).reshape(depth, SG_CHUNK, SG_HEADS * SG_CHUNK).astype(bf16),
        sg_bias=sg_bias,
        sg_norm=sg_norm.reshape(depth, 1, SG_WIDTH),
        w_pool=w_pool_bd.astype(bf16),
        pool_scale=pool_scale.reshape(depth, 1, POOL_WIDTH),
        w_o=w_o.astype(bf16),
        ffn_norm=ffn_norm.reshape(depth, 1, D_MODEL),
        w_gate=w_gate.astype(bf16),
        w_up=w_up.astype(bf16),
        w_down=w_down.astype(bf16),
    )


def kernel(x, positions, mix_norm, w_in, q_norm, kv_norm, w_uq, w_ukv, w_pool, pool_scale, sg_norm, w_s, b_s,
           w_o, ffn_norm, w_gate, w_up, w_down, final_norm):
    batch, seq, _ = x.shape
    depth = w_in.shape[0]
    assert seq % TM_PROJ == 0 and seq % TM_POST == 0 and seq % (TQ * Q_SUBTILES) == 0 and seq % TK == 0
    tables = _rope_tables(positions)
    x2 = x.reshape(batch * seq, D_MODEL)
    fin = final_norm.reshape(1, D_MODEL)
    w = _prep_weights(mix_norm, w_in, q_norm, kv_norm, w_uq, w_ukv, w_pool, pool_scale, sg_norm, w_s, b_s,
                      w_o, ffn_norm, w_gate, w_up, w_down)
    for l in range(depth):
        q_t, k, v_t, hp, c = _proj_call(x2, w, l, tables, batch, seq)
        a = _attn_call(q_t, k, v_t)
        x2 = _post_call(x2, a.reshape(batch * seq, MLA_WIDTH), hp, c, w, l, fin, seq, last=(l == depth - 1))
    return x2.reshape(batch, seq, D_MODEL)
```

```python
import functools

import jax
import jax.numpy as jnp
from jax import lax
from jax.experimental import pallas as pl
from jax.experimental.pallas import tpu as pltpu

D_MODEL = 1024
MLA_HEADS = 8
QK_NOPE_DIM = 64
QK_ROPE_DIM = 32
V_HEAD_DIM = 64
Q_LORA_RANK = 384
KV_LORA_RANK = 256
ROPE_THETA = 10000.0
MLA_WIDTH = MLA_HEADS * V_HEAD_DIM
POOL_WINDOWS = (2, 4, 8, 16)
POOL_GROUP_DIM = 64
POOL_WIDTH = len(POOL_WINDOWS) * POOL_GROUP_DIM
SG_HEADS = 4
SG_HEAD_DIM = 64
SG_WIDTH = SG_HEADS * SG_HEAD_DIM
SG_CHUNK = 128
D_FF = 2816
EPS = 1e-6
LOG2_E = 1.4426950408889634

LANES = 128
HEAD_PAD = 128
ROPE_OFF = QK_NOPE_DIM
BF16_ROWS = 16
V_PAD = V_HEAD_DIM + BF16_ROWS
POOL_HALO = 8
FF_CHUNK = 256

C_Q = 0
C_KV = C_Q + Q_LORA_RANK
C_POOL = C_KV + KV_LORA_RANK
C_SG = C_POOL + POOL_WIDTH
C_ROPE = C_SG + 2 * SG_WIDTH
IN_COLS_PAD = C_ROPE + LANES
QK_DIM = QK_NOPE_DIM + QK_ROPE_DIM

TM_PROJ = 512
TM_POST = 512
TQ = 512
TK = 256
Q_SUBTILES = 2
TM_ROPE = 2048

VMEM_LIMIT = 56 * 1024 * 1024

f32 = jnp.float32
bf16 = jnp.bfloat16


def _rms(x, g):
    return x * lax.rsqrt(jnp.mean(x * x, axis=-1, keepdims=True) + EPS) * g


def _dot(a, b):
    return jnp.dot(a, b, preferred_element_type=f32)


def _dot_nt(a, b):
    return lax.dot_general(a, b, (((1,), (1,)), ((), ())), preferred_element_type=f32)


def _rope_kernel(pos_ref, inv_ref, cos_t_ref, sin_t_ref, cos_k_ref, sin_k_ref):
    tm = pos_ref.shape[1]
    ang_t = pos_ref[...].astype(f32) * inv_ref[...]
    cos_t = jnp.cos(ang_t)
    sin_t = jnp.sin(ang_t)
    cos_t_ref[...] = cos_t
    sin_t_ref[...] = sin_t
    above = jnp.zeros((ROPE_OFF, tm), f32)
    below = jnp.zeros((HEAD_PAD - ROPE_OFF - QK_ROPE_DIM, tm), f32)
    cos_k_ref[...] = jnp.concatenate([above, cos_t, below], axis=0).T
    sin_k_ref[...] = jnp.concatenate([above, sin_t, below], axis=0).T


def _rope_tables(positions):
    n = positions.size
    inv_freq = ROPE_THETA ** (-jnp.arange(0, QK_ROPE_DIM, 2, dtype=f32) / QK_ROPE_DIM)
    inv2 = jnp.concatenate([inv_freq, inv_freq])
    tm = TM_ROPE
    return pl.pallas_call(
        _rope_kernel,
        grid=(n // tm,),
        in_specs=[
            pl.BlockSpec((1, tm), lambda i: (0, i)),
            pl.BlockSpec((QK_ROPE_DIM, 1), lambda i: (0, 0)),
        ],
        out_specs=[
            pl.BlockSpec((QK_ROPE_DIM, tm), lambda i: (0, i)),
            pl.BlockSpec((QK_ROPE_DIM, tm), lambda i: (0, i)),
            pl.BlockSpec((tm, HEAD_PAD), lambda i: (i, 0)),
            pl.BlockSpec((tm, HEAD_PAD), lambda i: (i, 0)),
        ],
        out_shape=[
            jax.ShapeDtypeStruct((QK_ROPE_DIM, n), f32),
            jax.ShapeDtypeStruct((QK_ROPE_DIM, n), f32),
            jax.ShapeDtypeStruct((n, HEAD_PAD), f32),
            jax.ShapeDtypeStruct((n, HEAD_PAD), f32),
        ],
        compiler_params=pltpu.CompilerParams(dimension_semantics=("parallel",)),
        name="rope_tables",
    )(positions.reshape(1, n), inv2.reshape(QK_ROPE_DIM, 1))


def _proj_kernel(x_ref, g_ref, win_ref, qn_ref, kvn_ref, wuq_t_ref, wk_ref, wuv_t_ref,
                 ws_ref, bias_ref, sgn_ref, cos_t_ref, sin_t_ref, cos_k_ref, sin_k_ref,
                 q_t_ref, k_ref, v_t_ref, hp_ref, c_ref):
    tm = x_ref.shape[0]
    scale = (QK_NOPE_DIM + QK_ROPE_DIM) ** -0.5 * LOG2_E
    h = _rms(x_ref[...], g_ref[...]).astype(bf16)
    p = _dot(h, win_ref[...])

    cq = _rms(p[:, C_Q:C_Q + Q_LORA_RANK], qn_ref[...]).astype(bf16)
    q_t = _dot_nt(wuq_t_ref[...], cq)
    cos_t = cos_t_ref[...]
    sin_t = sin_t_ref[...]
    half = QK_ROPE_DIM // 2
    for hd in range(MLA_HEADS):
        b0 = hd * QK_DIM
        o0 = hd * HEAD_PAD
        nope = q_t[b0:b0 + QK_NOPE_DIM]
        rope = q_t[b0 + QK_NOPE_DIM:b0 + QK_DIM]
        rot = jnp.concatenate([-rope[half:], rope[:half]], axis=0)
        q_t_ref[o0:o0 + QK_NOPE_DIM, :] = (nope * scale).astype(bf16)
        q_t_ref[o0 + ROPE_OFF:o0 + ROPE_OFF + QK_ROPE_DIM, :] = (
            (rope * cos_t + rot * sin_t) * scale).astype(bf16)
        q_t_ref[o0 + ROPE_OFF + QK_ROPE_DIM:o0 + HEAD_PAD, :] = jnp.zeros(
            (HEAD_PAD - ROPE_OFF - QK_ROPE_DIM, tm), bf16)

    ckv = _rms(p[:, C_KV:C_KV + KV_LORA_RANK], kvn_ref[...]).astype(bf16)
    k_nope = _dot(ckv, wk_ref[...])
    rope_blk = p[:, C_ROPE:C_ROPE + LANES]
    rot_blk = pltpu.roll(rope_blk, LANES - QK_ROPE_DIM, axis=1)
    k_rope = rope_blk * cos_k_ref[...] + rot_blk * sin_k_ref[...]
    for hd in range(MLA_HEADS):
        b0 = hd * HEAD_PAD
        k_ref[:, b0:b0 + HEAD_PAD] = (k_nope[:, b0:b0 + HEAD_PAD] + k_rope).astype(bf16)
    v_t = _dot_nt(wuv_t_ref[...], ckv).astype(bf16)
    for hd in range(MLA_HEADS):
        b0 = hd * V_PAD
        v_t_ref[b0:b0 + V_HEAD_DIM, :] = v_t[hd * V_HEAD_DIM:(hd + 1) * V_HEAD_DIM]
        v_t_ref[b0 + V_HEAD_DIM:b0 + V_PAD, :] = jnp.ones((V_PAD - V_HEAD_DIM, tm), bf16)

    hp_ref[...] = p[:, C_POOL:C_POOL + POOL_WIDTH]

    z = jax.nn.gelu(p[:, C_SG:C_SG + 2 * SG_WIDTH])
    u = z[:, :SG_WIDTH]
    v = z[:, SG_WIDTH:]
    grp = lax.broadcasted_iota(jnp.int32, (1, SG_WIDTH), 1) // SG_HEAD_DIM
    vsq = v * v
    inv = jnp.zeros_like(v)
    for g in range(SG_HEADS):
        sel = grp == g
        ms = jnp.sum(jnp.where(sel, vsq, 0.0), axis=-1, keepdims=True) / SG_HEAD_DIM
        inv = jnp.where(sel, lax.rsqrt(ms + EPS), inv)
    vn = (v * inv * sgn_ref[...]).astype(bf16)
    bias = bias_ref[...]
    ws_cat = ws_ref[...]
    zero = jnp.zeros((), bf16)
    for c in range(tm // SG_CHUNK):
        r0 = c * SG_CHUNK
        vc = vn[r0:r0 + SG_CHUNK]
        v_blocks = jnp.concatenate([jnp.where(grp == g, vc, zero) for g in range(SG_HEADS)], axis=0)
        mixed = _dot(ws_cat, v_blocks) + bias
        c_ref[r0:r0 + SG_CHUNK, :] = (u[r0:r0 + SG_CHUNK] * mixed).astype(bf16)


def _const_spec(shape):
    nd = len(shape)
    return pl.BlockSpec(shape, lambda *_: (0,) * nd)


def _layer_spec(shape, l, **kwargs):
    nd = len(shape)
    return pl.BlockSpec((None,) + tuple(shape), lambda *_: (l,) + (0,) * nd, **kwargs)


def _proj_call(x2, w, l, tables, batch, seq):
    n = x2.shape[0]
    tm = TM_PROJ
    cos_t, sin_t, cos_k, sin_k = tables
    row = lambda cols: pl.BlockSpec((tm, cols), lambda i: (i, 0))
    col = lambda rows: pl.BlockSpec((rows, tm), lambda i: (0, i))
    tiles_per_seq = seq // tm
    q_spec = pl.BlockSpec((None, MLA_HEADS * HEAD_PAD, tm), lambda i: (i // tiles_per_seq, 0, i % tiles_per_seq))
    v_spec = pl.BlockSpec((None, MLA_HEADS * V_PAD, tm), lambda i: (i // tiles_per_seq, 0, i % tiles_per_seq))
    k_spec = pl.BlockSpec((None, tm, MLA_HEADS * HEAD_PAD), lambda i: (i // tiles_per_seq, i % tiles_per_seq, 0))
    return pl.pallas_call(
        _proj_kernel,
        grid=(n // tm,),
        in_specs=[
            row(D_MODEL),
            _layer_spec((1, D_MODEL), l),
            _layer_spec((D_MODEL, IN_COLS_PAD), l),
            _layer_spec((1, Q_LORA_RANK), l),
            _layer_spec((1, KV_LORA_RANK), l),
            _layer_spec((MLA_HEADS * QK_DIM, Q_LORA_RANK), l),
            _layer_spec((KV_LORA_RANK, MLA_HEADS * HEAD_PAD), l),
            _layer_spec((MLA_WIDTH, KV_LORA_RANK), l),
            _layer_spec((SG_CHUNK, SG_HEADS * SG_CHUNK), l),
            _layer_spec((SG_CHUNK, SG_WIDTH), l),
            _layer_spec((1, SG_WIDTH), l),
            col(QK_ROPE_DIM),
            col(QK_ROPE_DIM),
            row(HEAD_PAD),
            row(HEAD_PAD),
        ],
        out_specs=[q_spec, k_spec, v_spec, row(POOL_WIDTH), row(SG_WIDTH)],
        out_shape=[
            jax.ShapeDtypeStruct((batch, MLA_HEADS * HEAD_PAD, seq), bf16),
            jax.ShapeDtypeStruct((batch, seq, MLA_HEADS * HEAD_PAD), bf16),
            jax.ShapeDtypeStruct((batch, MLA_HEADS * V_PAD, seq), bf16),
            jax.ShapeDtypeStruct((n, POOL_WIDTH), f32),
            jax.ShapeDtypeStruct((n, SG_WIDTH), bf16),
        ],
        compiler_params=pltpu.CompilerParams(dimension_semantics=("parallel",), vmem_limit_bytes=VMEM_LIMIT),
        name="proj",
    )(x2, w["mix_norm"], w["w_in"], w["q_norm"], w["kv_norm"], w["w_uq_t"], w["w_k"], w["w_uv_t"],
      w["w_s"], w["sg_bias"], w["sg_norm"], cos_t, sin_t, cos_k, sin_k)


def _attn_kernel(q_t_ref, k_ref, v_t_ref, o_ref, s_buf, p_buf, acc_ref, *, tq, tk):
    seq = k_ref.shape[0]
    n_kv = seq // tk
    assert n_kv >= 3
    n_sub = q_t_ref.shape[1] // tq

    def stream(sid, qs, hh):
        q_t = q_t_ref[hh * HEAD_PAD:(hh + 1) * HEAD_PAD, qs * tq:(qs + 1) * tq]

        def stage_a(t):
            s_t = _dot(k_ref[t * tk:(t + 1) * tk, hh * HEAD_PAD:(hh + 1) * HEAD_PAD], q_t)
            s_buf[sid, t % 2] = s_t
            return jnp.max(s_t, axis=0, keepdims=True)

        def stage_b(t, cmax, m):
            m_new = cmax if m is None else jnp.maximum(m, cmax)
            p_buf[sid, t % 2] = jnp.exp2(s_buf[sid, t % 2] - m_new).astype(bf16)
            alpha = None if m is None else jnp.exp2(m - m_new)
            return m_new, alpha

        def stage_c(t, alpha):
            v_c = v_t_ref[hh * V_PAD:(hh + 1) * V_PAD, t * tk:(t + 1) * tk]
            part = _dot(v_c, p_buf[sid, t % 2])
            acc_ref[sid] = part if alpha is None else alpha * acc_ref[sid] + part

        cmax, m, alphas = None, None, {}
        for t in range(n_kv + 2):
            cmax_new = stage_a(t) if t < n_kv else None
            if 1 <= t <= n_kv:
                m, alphas[t - 1] = stage_b(t - 1, cmax, m)
            if t >= 2:
                stage_c(t - 2, alphas.pop(t - 2))
            cmax = cmax_new
        acc = acc_ref[sid]
        return acc[:V_HEAD_DIM] / acc[V_HEAD_DIM:V_HEAD_DIM + 1]

    for qs in range(n_sub):
        o_t = jnp.concatenate([stream(2 * qs + hh, qs, hh) for hh in range(2)], axis=0)
        o_ref[qs * tq:(qs + 1) * tq, :] = o_t.T.astype(o_ref.dtype)


def _attn_call(q_t, k, v_t):
    batch, _, seq = q_t.shape
    pairs = MLA_HEADS // 2
    n_streams = 2 * Q_SUBTILES
    tq_blk = TQ * Q_SUBTILES
    return pl.pallas_call(
        functools.partial(_attn_kernel, tq=TQ, tk=TK),
        grid=(batch, pairs, seq // tq_blk),
        in_specs=[
            pl.BlockSpec((None, 2 * HEAD_PAD, tq_blk), lambda b, h, q: (b, h, q)),
            pl.BlockSpec((None, seq, 2 * HEAD_PAD), lambda b, h, q: (b, 0, h)),
            pl.BlockSpec((None, 2 * V_PAD, seq), lambda b, h, q: (b, h, 0)),
        ],
        out_specs=pl.BlockSpec((None, tq_blk, 2 * V_HEAD_DIM), lambda b, h, q: (b, q, h)),
        out_shape=jax.ShapeDtypeStruct((batch, seq, MLA_WIDTH), bf16),
        scratch_shapes=[pltpu.VMEM((n_streams, 2, TK, TQ), f32), pltpu.VMEM((n_streams, 2, TK, TQ), bf16),
                        pltpu.VMEM((n_streams, V_PAD, TQ), f32)],
        compiler_params=pltpu.CompilerParams(
            dimension_semantics=("parallel", "parallel", "arbitrary"), vmem_limit_bytes=VMEM_LIMIT),
        name="attn",
    )(q_t, k, v_t)


def _post_mix(x_ref, a_ref, hp_ref, hp_prev_ref, hp_next_ref, c_ref, wo_ref, wpool_ref, pscale_ref, fn_ref,
              ext_ref, a1_ref, a2_ref, a4_ref, *, si, seq):
    tm = x_ref.shape[0]
    tiles_per_seq = seq // tm
    hp = hp_ref[...]
    assert POOL_WINDOWS == (2, 4, 8, 16) and POOL_WIDTH == 2 * LANES

    h8 = POOL_HALO
    t0 = 2 * h8
    rows = tm + 3 * h8
    zero_rows = jnp.zeros((h8, POOL_WIDTH), f32)
    ext_ref[0:h8, :] = zero_rows
    ext_ref[h8:t0, :] = jnp.where(si > 0, hp_prev_ref[...], 0.0)
    ext_ref[t0:t0 + tm, :] = hp
    ext_ref[t0 + tm:rows, :] = jnp.where(si < tiles_per_seq - 1, hp_next_ref[...], 0.0)
    a1_ref[0:h8, :] = zero_rows
    a2_ref[0:h8, :] = zero_rows[:, :LANES]
    a1_ref[h8:rows, :] = ext_ref[h8:rows, :] + ext_ref[h8 - 1:rows - 1, :]
    a2_ref[h8:rows, :] = a1_ref[h8:rows, LANES:] + a1_ref[h8 - 2:rows - 2, LANES:]
    a4_ref[h8:rows, :] = a2_ref[h8:rows, :] + a2_ref[h8 - 4:rows - 4, :]

    def window(ref, lanes, half_w):
        end = t0 + half_w - 1
        return ref[end:end + tm, lanes] + ref[end - half_w:end - half_w + tm, lanes]

    lo = slice(0, LANES)
    sums = [a1_ref[t0:t0 + tm, lo], window(a1_ref, lo, 2), window(a2_ref, slice(None), 4),
            window(a4_ref, slice(None), 8)]
    t_abs = si * tm + lax.broadcasted_iota(jnp.int32, (tm, 1), 0)
    counts = []
    for w in POOL_WINDOWS:
        left = w // 2
        right = w - 1 - left
        counts.append((jnp.minimum(t_abs + right + 1, seq) - jnp.maximum(t_abs - left, 0)).astype(f32))
    first = lax.broadcasted_iota(jnp.int32, (1, LANES), 1) < POOL_GROUP_DIM
    mean = jnp.concatenate(
        [jnp.where(first, sums[0], sums[1]) / jnp.where(first, counts[0], counts[1]),
         jnp.where(first, sums[2], sums[3]) / jnp.where(first, counts[2], counts[3])], axis=1)
    d = (mean - hp).astype(bf16)
    b = _dot(d, wpool_ref[...]) * pscale_ref[...]

    mixo = (_dot(a_ref[...], wo_ref[0:MLA_WIDTH, :])
            + _dot(b.astype(bf16), wo_ref[MLA_WIDTH:MLA_WIDTH + POOL_WIDTH, :])
            + _dot(c_ref[...], wo_ref[MLA_WIDTH + POOL_WIDTH:, :]))
    x1 = x_ref[...] + mixo
    return x1, _rms(x1, fn_ref[...]).astype(bf16)


def _post_ffn(x1, hn, wg_ref, wu_ref, wd_ref, final_ref, *, last):
    y = None
    for f in range(D_FF // FF_CHUNK):
        f0 = f * FF_CHUNK
        gte = _dot(hn, wg_ref[:, f0:f0 + FF_CHUNK])
        up = _dot(hn, wu_ref[:, f0:f0 + FF_CHUNK])
        act = (jax.nn.silu(gte) * up).astype(bf16)
        part = _dot(act, wd_ref[f0:f0 + FF_CHUNK, :])
        y = part if y is None else y + part
    out = x1 + y
    return _rms(out, final_ref[...]) if last else out


def _post_kernel(x_ref, a_ref, hp_ref, hp_prev_ref, hp_next_ref, c_ref, wo_ref, wpool_ref, pscale_ref,
                 fn_ref, wg_ref, wu_ref, wd_ref, final_ref, o_ref, ext_ref, a1_ref, a2_ref, a4_ref, *, seq, last):
    tm = x_ref.shape[0]
    si = pl.program_id(0) % (seq // tm)
    x1, hn = _post_mix(x_ref, a_ref, hp_ref, hp_prev_ref, hp_next_ref, c_ref, wo_ref, wpool_ref, pscale_ref,
                       fn_ref, ext_ref, a1_ref, a2_ref, a4_ref, si=si, seq=seq)
    o_ref[...] = _post_ffn(x1, hn, wg_ref, wu_ref, wd_ref, final_ref, last=last)


def _post_call(x2, a2, hp, c, w, l, final_norm, seq, last):
    n = x2.shape[0]
    tm = TM_POST
    hb = tm // POOL_HALO
    n_halo_blocks = n // POOL_HALO
    ext_rows = tm + 3 * POOL_HALO
    row = lambda cols: pl.BlockSpec((tm, cols), lambda i: (i, 0))
    resident = lambda shape: _layer_spec(shape, l, pipeline_mode=pl.Buffered(1))
    return pl.pallas_call(
        functools.partial(_post_kernel, seq=seq, last=last),
        grid=(n // tm,),
        in_specs=[
            row(D_MODEL),
            row(MLA_WIDTH),
            row(POOL_WIDTH),
            pl.BlockSpec((POOL_HALO, POOL_WIDTH), lambda i: (jnp.maximum(i * hb - 1, 0), 0)),
            pl.BlockSpec((POOL_HALO, POOL_WIDTH), lambda i: (jnp.minimum((i + 1) * hb, n_halo_blocks - 1), 0)),
            row(SG_WIDTH),
            resident((D_MODEL, D_MODEL)),
            resident((POOL_WIDTH, POOL_WIDTH)),
            _layer_spec((1, POOL_WIDTH), l),
            _layer_spec((1, D_MODEL), l),
            resident((D_MODEL, D_FF)),
            resident((D_MODEL, D_FF)),
            resident((D_FF, D_MODEL)),
            _const_spec((1, D_MODEL)),
        ],
        out_specs=row(D_MODEL),
        out_shape=jax.ShapeDtypeStruct((n, D_MODEL), f32),
        scratch_shapes=[pltpu.VMEM((ext_rows, POOL_WIDTH), f32), pltpu.VMEM((ext_rows, POOL_WIDTH), f32),
                        pltpu.VMEM((ext_rows, LANES), f32), pltpu.VMEM((ext_rows, LANES), f32)],
        compiler_params=pltpu.CompilerParams(dimension_semantics=("parallel",), vmem_limit_bytes=VMEM_LIMIT),
        name="post_last" if last else "post",
    )(x2, a2, hp, hp, hp, c, w["w_o"], w["w_pool"], w["pool_scale"], w["ffn_norm"],
      w["w_gate"], w["w_up"], w["w_down"], final_norm)


def _prep_weights(mix_norm, w_in, q_norm, kv_norm, w_uq, w_ukv, w_pool, pool_scale, sg_norm, w_s, b_s,
                  w_o, ffn_norm, w_gate, w_up, w_down):
    depth = w_in.shape[0]
    o1 = Q_LORA_RANK
    o2 = o1 + KV_LORA_RANK
    o3 = o2 + QK_ROPE_DIM
    half = QK_ROPE_DIM // 2
    zeros = jnp.zeros
    w_in, w_uq, w_ukv = w_in.astype(bf16), w_uq.astype(bf16), w_ukv.astype(bf16)

    w_kr = w_in[:, :, o2:o3]
    w_in_pad = jnp.concatenate([w_in[:, :, :o2], w_in[:, :, o3:], zeros((depth, D_MODEL, ROPE_OFF), bf16),
                                w_kr, -w_kr[..., half:], w_kr[..., :half]], axis=-1)

    w_uq_t = jnp.swapaxes(w_uq, 1, 2)

    wkv = w_ukv.reshape(depth, KV_LORA_RANK, MLA_HEADS, QK_NOPE_DIM + V_HEAD_DIM)
    w_k = jnp.concatenate([wkv[..., :QK_NOPE_DIM],
                           zeros((depth, KV_LORA_RANK, MLA_HEADS, HEAD_PAD - QK_NOPE_DIM), bf16)], axis=-1)
    w_k = w_k.reshape(depth, KV_LORA_RANK, MLA_HEADS * HEAD_PAD)
    w_uv_t = jnp.swapaxes(wkv[..., QK_NOPE_DIM:].reshape(depth, KV_LORA_RANK, MLA_WIDTH), 1, 2)

    n_groups = len(POOL_WINDOWS)
    eye = jnp.eye(n_groups, dtype=f32)
    w_pool_bd = (eye[None, :, None, :, None] * w_pool[:, :, :, None, :]).reshape(depth, POOL_WIDTH, POOL_WIDTH)

    sg_bias = jnp.repeat(jnp.swapaxes(b_s, 1, 2), SG_HEAD_DIM, axis=2)
    return dict(
        mix_norm=mix_norm.reshape(depth, 1, D_MODEL),
        w_in=w_in_pad,
        q_norm=q_norm.reshape(depth, 1, Q_LORA_RANK),
        kv_norm=kv_norm.reshape(depth, 1, KV_LORA_RANK),
        w_uq_t=w_uq_t,
        w_k=w_k,
        w_uv_t=w_uv_t,
        w_s=jnp.swapaxes(w_s, 1, 2).reshape(depth, SG_CHUNK, SG_HEADS * SG_CHUNK).astype(bf16),
        sg_bias=sg_bias,
        sg_norm=sg_norm.reshape(depth, 1, SG_WIDTH),
        w_pool=w_pool_bd.astype(bf16),
        pool_scale=pool_scale.reshape(depth, 1, POOL_WIDTH),
        w_o=w_o.astype(bf16),
        ffn_norm=ffn_norm.reshape(depth, 1, D_MODEL),
        w_gate=w_gate.astype(bf16),
        w_up=w_up.astype(bf16),
        w_down=w_down.astype(bf16),
    )


def kernel(x, positions, mix_norm, w_in, q_norm, kv_norm, w_uq, w_ukv, w_pool, pool_scale, sg_norm, w_s, b_s,
           w_o, ffn_norm, w_gate, w_up, w_down, final_norm):
    batch, seq, _ = x.shape
    depth = w_in.shape[0]
    assert seq % TM_PROJ == 0 and seq % TM_POST == 0 and seq % (TQ * Q_SUBTILES) == 0 and seq % TK == 0
    tables = _rope_tables(positions)
    x2 = x.reshape(batch * seq, D_MODEL)
    fin = final_norm.reshape(1, D_MODEL)
    w = _prep_weights(mix_norm, w_in, q_norm, kv_norm, w_uq, w_ukv, w_pool, pool_scale, sg_norm, w_s, b_s,
                      w_o, ffn_norm, w_gate, w_up, w_down)
    for l in range(depth):
        q_t, k, v_t, hp, c = _proj_call(x2, w, l, tables, batch, seq)
        a = _attn_call(q_t, k, v_t)
        x2 = _post_call(x2, a.reshape(batch * seq, MLA_WIDTH), hp, c, w, l, fin, seq, last=(l == depth - 1))
    return x2.reshape(batch, seq, D_MODEL)
```

```python
import functools

import jax
import jax.numpy as jnp
from jax import lax
from jax.experimental import pallas as pl
from jax.experimental.pallas import tpu as pltpu

D_MODEL = 1024
MLA_HEADS = 8
QK_NOPE_DIM = 64
QK_ROPE_DIM = 32
V_HEAD_DIM = 64
Q_LORA_RANK = 384
KV_LORA_RANK = 256
ROPE_THETA = 10000.0
MLA_WIDTH = MLA_HEADS * V_HEAD_DIM
POOL_WINDOWS = (2, 4, 8, 16)
POOL_GROUP_DIM = 64
POOL_WIDTH = len(POOL_WINDOWS) * POOL_GROUP_DIM
SG_HEADS = 4
SG_HEAD_DIM = 64
SG_WIDTH = SG_HEADS * SG_HEAD_DIM
SG_CHUNK = 128
D_FF = 2816
EPS = 1e-6
LOG2_E = 1.4426950408889634

LANES = 128
HEAD_PAD = 128
ROPE_OFF = QK_NOPE_DIM
BF16_ROWS = 16
V_PAD = V_HEAD_DIM + BF16_ROWS
POOL_HALO = 8
FF_CHUNK = 256

C_Q = 0
C_KV = C_Q + Q_LORA_RANK
C_POOL = C_KV + KV_LORA_RANK
C_SG = C_POOL + POOL_WIDTH
C_ROPE = C_SG + 2 * SG_WIDTH
IN_COLS_PAD = C_ROPE + LANES
QK_DIM = QK_NOPE_DIM + QK_ROPE_DIM

TM_PROJ = 512
TM_POST = 512
TQ = 512
TK = 256
Q_SUBTILES = 2
TM_ROPE = 2048

V7X_VMEM_BYTES = 64 * 1024 * 1024
VMEM_LIMIT = V7X_VMEM_BYTES * 7 // 8

f32 = jnp.float32
bf16 = jnp.bfloat16


def _rms(x, g):
    return x * lax.rsqrt(jnp.mean(x * x, axis=-1, keepdims=True) + EPS) * g


def _dot(a, b):
    return jnp.dot(a, b, preferred_element_type=f32)


def _dot_nt(a, b):
    return lax.dot_general(a, b, (((1,), (1,)), ((), ())), preferred_element_type=f32)


def _rope_kernel(pos_ref, inv_ref, cos_t_ref, sin_t_ref, cos_k_ref, sin_k_ref):
    tm = pos_ref.shape[1]
    ang_t = pos_ref[...].astype(f32) * inv_ref[...]
    cos_t = jnp.cos(ang_t)
    sin_t = jnp.sin(ang_t)
    cos_t_ref[...] = cos_t
    sin_t_ref[...] = sin_t
    above = jnp.zeros((ROPE_OFF, tm), f32)
    below = jnp.zeros((HEAD_PAD - ROPE_OFF - QK_ROPE_DIM, tm), f32)
    cos_k_ref[...] = jnp.concatenate([above, cos_t, below], axis=0).T
    sin_k_ref[...] = jnp.concatenate([above, sin_t, below], axis=0).T


def _rope_tables(positions):
    n = positions.size
    inv_freq = ROPE_THETA ** (-jnp.arange(0, QK_ROPE_DIM, 2, dtype=f32) / QK_ROPE_DIM)
    inv2 = jnp.concatenate([inv_freq, inv_freq])
    tm = TM_ROPE
    return pl.pallas_call(
        _rope_kernel,
        grid=(n // tm,),
        in_specs=[
            pl.BlockSpec((1, tm), lambda i: (0, i)),
            pl.BlockSpec((QK_ROPE_DIM, 1), lambda i: (0, 0)),
        ],
        out_specs=[
            pl.BlockSpec((QK_ROPE_DIM, tm), lambda i: (0, i)),
            pl.BlockSpec((QK_ROPE_DIM, tm), lambda i: (0, i)),
            pl.BlockSpec((tm, HEAD_PAD), lambda i: (i, 0)),
            pl.BlockSpec((tm, HEAD_PAD), lambda i: (i, 0)),
        ],
        out_shape=[
            jax.ShapeDtypeStruct((QK_ROPE_DIM, n), f32),
            jax.ShapeDtypeStruct((QK_ROPE_DIM, n), f32),
            jax.ShapeDtypeStruct((n, HEAD_PAD), f32),
            jax.ShapeDtypeStruct((n, HEAD_PAD), f32),
        ],
        compiler_params=pltpu.CompilerParams(dimension_semantics=("parallel",)),
        name="rope_tables",
    )(positions.reshape(1, n), inv2.reshape(QK_ROPE_DIM, 1))


def _proj_kernel(x_ref, g_ref, win_ref, qn_ref, kvn_ref, wuq_t_ref, wk_ref, wuv_t_ref,
                 ws_ref, bias_ref, sgn_ref, cos_t_ref, sin_t_ref, cos_k_ref, sin_k_ref,
                 q_t_ref, k_ref, v_t_ref, hp_ref, c_ref):
    tm = x_ref.shape[0]
    scale = (QK_NOPE_DIM + QK_ROPE_DIM) ** -0.5 * LOG2_E
    h = _rms(x_ref[...], g_ref[...]).astype(bf16)
    p = _dot(h, win_ref[...])

    cq = _rms(p[:, C_Q:C_Q + Q_LORA_RANK], qn_ref[...]).astype(bf16)
    q_t = _dot_nt(wuq_t_ref[...], cq)
    cos_t = cos_t_ref[...]
    sin_t = sin_t_ref[...]
    half = QK_ROPE_DIM // 2
    for hd in range(MLA_HEADS):
        b0 = hd * QK_DIM
        o0 = hd * HEAD_PAD
        nope = q_t[b0:b0 + QK_NOPE_DIM]
        rope = q_t[b0 + QK_NOPE_DIM:b0 + QK_DIM]
        rot = jnp.concatenate([-rope[half:], rope[:half]], axis=0)
        q_t_ref[o0:o0 + QK_NOPE_DIM, :] = (nope * scale).astype(bf16)
        q_t_ref[o0 + ROPE_OFF:o0 + ROPE_OFF + QK_ROPE_DIM, :] = (
            (rope * cos_t + rot * sin_t) * scale).astype(bf16)
        q_t_ref[o0 + ROPE_OFF + QK_ROPE_DIM:o0 + HEAD_PAD, :] = jnp.zeros(
            (HEAD_PAD - ROPE_OFF - QK_ROPE_DIM, tm), bf16)

    ckv = _rms(p[:, C_KV:C_KV + KV_LORA_RANK], kvn_ref[...]).astype(bf16)
    k_nope = _dot(ckv, wk_ref[...])
    rope_blk = p[:, C_ROPE:C_ROPE + LANES]
    rot_blk = pltpu.roll(rope_blk, LANES - QK_ROPE_DIM, axis=1)
    k_rope = rope_blk * cos_k_ref[...] + rot_blk * sin_k_ref[...]
    for hd in range(MLA_HEADS):
        b0 = hd * HEAD_PAD
        k_ref[:, b0:b0 + HEAD_PAD] = (k_nope[:, b0:b0 + HEAD_PAD] + k_rope).astype(bf16)
    v_t = _dot_nt(wuv_t_ref[...], ckv).astype(bf16)
    for hd in range(MLA_HEADS):
        b0 = hd * V_PAD
        v_t_ref[b0:b0 + V_HEAD_DIM, :] = v_t[hd * V_HEAD_DIM:(hd + 1) * V_HEAD_DIM]
        v_t_ref[b0 + V_HEAD_DIM:b0 + V_PAD, :] = jnp.ones((V_PAD - V_HEAD_DIM, tm), bf16)

    hp_ref[...] = p[:, C_POOL:C_POOL + POOL_WIDTH]

    z = jax.nn.gelu(p[:, C_SG:C_SG + 2 * SG_WIDTH])
    u = z[:, :SG_WIDTH]
    v = z[:, SG_WIDTH:]
    grp = lax.broadcasted_iota(jnp.int32, (1, SG_WIDTH), 1) // SG_HEAD_DIM
    vsq = v * v
    inv = jnp.zeros_like(v)
    for g in range(SG_HEADS):
        sel = grp == g
        ms = jnp.sum(jnp.where(sel, vsq, 0.0), axis=-1, keepdims=True) / SG_HEAD_DIM
        inv = jnp.where(sel, lax.rsqrt(ms + EPS), inv)
    vn = (v * inv * sgn_ref[...]).astype(bf16)
    bias = bias_ref[...]
    ws_cat = ws_ref[...]
    zero = jnp.zeros((), bf16)
    for c in range(tm // SG_CHUNK):
        r0 = c * SG_CHUNK
        vc = vn[r0:r0 + SG_CHUNK]
        v_blocks = jnp.concatenate([jnp.where(grp == g, vc, zero) for g in range(SG_HEADS)], axis=0)
        mixed = _dot(ws_cat, v_blocks) + bias
        c_ref[r0:r0 + SG_CHUNK, :] = (u[r0:r0 + SG_CHUNK] * mixed).astype(bf16)


def _const_spec(shape):
    nd = len(shape)
    return pl.BlockSpec(shape, lambda *_: (0,) * nd)


def _layer_spec(shape, l, **kwargs):
    nd = len(shape)
    return pl.BlockSpec((None,) + tuple(shape), lambda *_: (l,) + (0,) * nd, **kwargs)


def _proj_call(x2, w, l, tables, batch, seq):
    n = x2.shape[0]
    tm = TM_PROJ
    cos_t, sin_t, cos_k, sin_k = tables
    row = lambda cols: pl.BlockSpec((tm, cols), lambda i: (i, 0))
    col = lambda rows: pl.BlockSpec((rows, tm), lambda i: (0, i))
    tiles_per_seq = seq // tm
    q_spec = pl.BlockSpec((None, MLA_HEADS * HEAD_PAD, tm), lambda i: (i // tiles_per_seq, 0, i % tiles_per_seq))
    v_spec = pl.BlockSpec((None, MLA_HEADS * V_PAD, tm), lambda i: (i // tiles_per_seq, 0, i % tiles_per_seq))
    k_spec = pl.BlockSpec((None, tm, MLA_HEADS * HEAD_PAD), lambda i: (i // tiles_per_seq, i % tiles_per_seq, 0))
    return pl.pallas_call(
        _proj_kernel,
        grid=(n // tm,),
        in_specs=[
            row(D_MODEL),
            _layer_spec((1, D_MODEL), l),
            _layer_spec((D_MODEL, IN_COLS_PAD), l),
            _layer_spec((1, Q_LORA_RANK), l),
            _layer_spec((1, KV_LORA_RANK), l),
            _layer_spec((MLA_HEADS * QK_DIM, Q_LORA_RANK), l),
            _layer_spec((KV_LORA_RANK, MLA_HEADS * HEAD_PAD), l),
            _layer_spec((MLA_WIDTH, KV_LORA_RANK), l),
            _layer_spec((SG_CHUNK, SG_HEADS * SG_CHUNK), l),
            _layer_spec((SG_CHUNK, SG_WIDTH), l),
            _layer_spec((1, SG_WIDTH), l),
            col(QK_ROPE_DIM),
            col(QK_ROPE_DIM),
            row(HEAD_PAD),
            row(HEAD_PAD),
        ],
        out_specs=[q_spec, k_spec, v_spec, row(POOL_WIDTH), row(SG_WIDTH)],
        out_shape=[
            jax.ShapeDtypeStruct((batch, MLA_HEADS * HEAD_PAD, seq), bf16),
            jax.ShapeDtypeStruct((batch, seq, MLA_HEADS * HEAD_PAD), bf16),
            jax.ShapeDtypeStruct((batch, MLA_HEADS * V_PAD, seq), bf16),
            jax.ShapeDtypeStruct((n, POOL_WIDTH), f32),
            jax.ShapeDtypeStruct((n, SG_WIDTH), bf16),
        ],
        compiler_params=pltpu.CompilerParams(dimension_semantics=("parallel",), vmem_limit_bytes=VMEM_LIMIT),
        name="proj",
    )(x2, w["mix_norm"], w["w_in"], w["q_norm"], w["kv_norm"], w["w_uq_t"], w["w_k"], w["w_uv_t"],
      w["w_s"], w["sg_bias"], w["sg_norm"], cos_t, sin_t, cos_k, sin_k)


def _attn_kernel(q_t_ref, k_ref, v_t_ref, o_ref, s_buf, p_buf, acc_ref, *, tq, tk):
    seq = k_ref.shape[0]
    n_kv = seq // tk
    assert n_kv >= 3
    n_sub = q_t_ref.shape[1] // tq

    def stream(sid, qs, hh):
        q_t = q_t_ref[hh * HEAD_PAD:hh * HEAD_PAD + QK_DIM, qs * tq:(qs + 1) * tq]

        def stage_a(t):
            s_t = _dot(k_ref[t * tk:(t + 1) * tk, hh * HEAD_PAD:hh * HEAD_PAD + QK_DIM], q_t)
            s_buf[sid, t % 2] = s_t
            return jnp.max(s_t, axis=0, keepdims=True)

        def stage_b(t, cmax, m):
            m_new = cmax if m is None else jnp.maximum(m, cmax)
            p_buf[sid, t % 2] = jnp.exp2(s_buf[sid, t % 2] - m_new).astype(bf16)
            alpha = None if m is None else jnp.exp2(m - m_new)
            return m_new, alpha

        def stage_c(t, alpha):
            v_c = v_t_ref[hh * V_PAD:(hh + 1) * V_PAD, t * tk:(t + 1) * tk]
            part = _dot(v_c, p_buf[sid, t % 2])
            acc_ref[sid] = part if alpha is None else alpha * acc_ref[sid] + part

        cmax, m, alphas = None, None, {}
        for t in range(n_kv + 2):
            cmax_new = stage_a(t) if t < n_kv else None
            if 1 <= t <= n_kv:
                m, alphas[t - 1] = stage_b(t - 1, cmax, m)
            if t >= 2:
                stage_c(t - 2, alphas.pop(t - 2))
            cmax = cmax_new
        acc = acc_ref[sid]
        return acc[:V_HEAD_DIM] / acc[V_HEAD_DIM:V_HEAD_DIM + 1]

    for qs in range(n_sub):
        o_t = jnp.concatenate([stream(2 * qs + hh, qs, hh) for hh in range(2)], axis=0)
        o_ref[qs * tq:(qs + 1) * tq, :] = o_t.T.astype(o_ref.dtype)


def _attn_call(q_t, k, v_t):
    batch, _, seq = q_t.shape
    pairs = MLA_HEADS // 2
    n_streams = 2 * Q_SUBTILES
    tq_blk = TQ * Q_SUBTILES
    return pl.pallas_call(
        functools.partial(_attn_kernel, tq=TQ, tk=TK),
        grid=(batch, pairs, seq // tq_blk),
        in_specs=[
            pl.BlockSpec((None, 2 * HEAD_PAD, tq_blk), lambda b, h, q: (b, h, q)),
            pl.BlockSpec((None, seq, 2 * HEAD_PAD), lambda b, h, q: (b, 0, h)),
            pl.BlockSpec((None, 2 * V_PAD, seq), lambda b, h, q: (b, h, 0)),
        ],
        out_specs=pl.BlockSpec((None, tq_blk, 2 * V_HEAD_DIM), lambda b, h, q: (b, q, h)),
        out_shape=jax.ShapeDtypeStruct((batch, seq, MLA_WIDTH), bf16),
        scratch_shapes=[pltpu.VMEM((n_streams, 2, TK, TQ), f32), pltpu.VMEM((n_streams, 2, TK, TQ), bf16),
                        pltpu.VMEM((n_streams, V_PAD, TQ), f32)],
        compiler_params=pltpu.CompilerParams(
            dimension_semantics=("parallel", "parallel", "arbitrary"), vmem_limit_bytes=VMEM_LIMIT),
        name="attn",
    )(q_t, k, v_t)


def _post_mix(x_ref, a_ref, hp_ref, hp_prev_ref, hp_next_ref, c_ref, wo_ref, wpool_ref, pscale_ref, fn_ref,
              ext_ref, a1_ref, a2_ref, a4_ref, *, si, seq):
    tm = x_ref.shape[0]
    tiles_per_seq = seq // tm
    hp = hp_ref[...]
    assert POOL_WINDOWS == (2, 4, 8, 16) and POOL_WIDTH == 2 * LANES

    h8 = POOL_HALO
    t0 = 2 * h8
    rows = tm + 3 * h8
    zero_rows = jnp.zeros((h8, POOL_WIDTH), f32)
    ext_ref[0:h8, :] = zero_rows
    ext_ref[h8:t0, :] = jnp.where(si > 0, hp_prev_ref[...], 0.0)
    ext_ref[t0:t0 + tm, :] = hp
    ext_ref[t0 + tm:rows, :] = jnp.where(si < tiles_per_seq - 1, hp_next_ref[...], 0.0)
    a1_ref[0:h8, :] = zero_rows
    a2_ref[0:h8, :] = zero_rows[:, :LANES]
    a1_ref[h8:rows, :] = ext_ref[h8:rows, :] + ext_ref[h8 - 1:rows - 1, :]
    a2_ref[h8:rows, :] = a1_ref[h8:rows, LANES:] + a1_ref[h8 - 2:rows - 2, LANES:]
    a4_ref[h8:rows, :] = a2_ref[h8:rows, :] + a2_ref[h8 - 4:rows - 4, :]

    def window(ref, lanes, half_w):
        end = t0 + half_w - 1
        return ref[end:end + tm, lanes] + ref[end - half_w:end - half_w + tm, lanes]

    lo = slice(0, LANES)
    sums = [a1_ref[t0:t0 + tm, lo], window(a1_ref, lo, 2), window(a2_ref, slice(None), 4),
            window(a4_ref, slice(None), 8)]
    t_abs = si * tm + lax.broadcasted_iota(jnp.int32, (tm, 1), 0)
    counts = []
    for w in POOL_WINDOWS:
        left = w // 2
        right = w - 1 - left
        counts.append((jnp.minimum(t_abs + right + 1, seq) - jnp.maximum(t_abs - left, 0)).astype(f32))
    first = lax.broadcasted_iota(jnp.int32, (1, LANES), 1) < POOL_GROUP_DIM
    mean = jnp.concatenate(
        [jnp.where(first, sums[0], sums[1]) / jnp.where(first, counts[0], counts[1]),
         jnp.where(first, sums[2], sums[3]) / jnp.where(first, counts[2], counts[3])], axis=1)
    d = (mean - hp).astype(bf16)
    b = _dot(d, wpool_ref[...]) * pscale_ref[...]

    mix = jnp.concatenate([a_ref[...], b.astype(bf16), c_ref[...]], axis=1)
    x1 = x_ref[...] + _dot(mix, wo_ref[...])
    return x1, _rms(x1, fn_ref[...]).astype(bf16)


def _post_ffn(x1, hn, wg_ref, wu_ref, wd_ref, final_ref, *, last):
    y = None
    for f in range(D_FF // FF_CHUNK):
        f0 = f * FF_CHUNK
        gte = _dot(hn, wg_ref[:, f0:f0 + FF_CHUNK])
        up = _dot(hn, wu_ref[:, f0:f0 + FF_CHUNK])
        act = (jax.nn.silu(gte) * up).astype(bf16)
        part = _dot(act, wd_ref[f0:f0 + FF_CHUNK, :])
        y = part if y is None else y + part
    out = x1 + y
    return _rms(out, final_ref[...]) if last else out


def _post_kernel(x_ref, a_ref, hp_ref, hp_prev_ref, hp_next_ref, c_ref, wo_ref, wpool_ref, pscale_ref,
                 fn_ref, wg_ref, wu_ref, wd_ref, final_ref, o_ref, ext_ref, a1_ref, a2_ref, a4_ref, *, seq, last):
    tm = x_ref.shape[0]
    si = pl.program_id(0) % (seq // tm)
    x1, hn = _post_mix(x_ref, a_ref, hp_ref, hp_prev_ref, hp_next_ref, c_ref, wo_ref, wpool_ref, pscale_ref,
                       fn_ref, ext_ref, a1_ref, a2_ref, a4_ref, si=si, seq=seq)
    o_ref[...] = _post_ffn(x1, hn, wg_ref, wu_ref, wd_ref, final_ref, last=last)


def _post_call(x2, a2, hp, c, w, l, final_norm, seq, last):
    n = x2.shape[0]
    tm = TM_POST
    hb = tm // POOL_HALO
    n_halo_blocks = n // POOL_HALO
    ext_rows = tm + 3 * POOL_HALO
    row = lambda cols: pl.BlockSpec((tm, cols), lambda i: (i, 0))
    resident = lambda shape: _layer_spec(shape, l, pipeline_mode=pl.Buffered(1))
    return pl.pallas_call(
        functools.partial(_post_kernel, seq=seq, last=last),
        grid=(n // tm,),
        in_specs=[
            row(D_MODEL),
            row(MLA_WIDTH),
            row(POOL_WIDTH),
            pl.BlockSpec((POOL_HALO, POOL_WIDTH), lambda i: (jnp.maximum(i * hb - 1, 0), 0)),
            pl.BlockSpec((POOL_HALO, POOL_WIDTH), lambda i: (jnp.minimum((i + 1) * hb, n_halo_blocks - 1), 0)),
            row(SG_WIDTH),
            resident((D_MODEL, D_MODEL)),
            resident((POOL_WIDTH, POOL_WIDTH)),
            _layer_spec((1, POOL_WIDTH), l),
            _layer_spec((1, D_MODEL), l),
            resident((D_MODEL, D_FF)),
            resident((D_MODEL, D_FF)),
            resident((D_FF, D_MODEL)),
            _const_spec((1, D_MODEL)),
        ],
        out_specs=row(D_MODEL),
        out_shape=jax.ShapeDtypeStruct((n, D_MODEL), f32),
        scratch_shapes=[pltpu.VMEM((ext_rows, POOL_WIDTH), f32), pltpu.VMEM((ext_rows, POOL_WIDTH), f32),
                        pltpu.VMEM((ext_rows, LANES), f32), pltpu.VMEM((ext_rows, LANES), f32)],
        compiler_params=pltpu.CompilerParams(dimension_semantics=("parallel",), vmem_limit_bytes=VMEM_LIMIT),
        name="post_last" if last else "post",
    )(x2, a2, hp, hp, hp, c, w["w_o"], w["w_pool"], w["pool_scale"], w["ffn_norm"],
      w["w_gate"], w["w_up"], w["w_down"], final_norm)


def _prep_weights(mix_norm, w_in, q_norm, kv_norm, w_uq, w_ukv, w_pool, pool_scale, sg_norm, w_s, b_s,
                  w_o, ffn_norm, w_gate, w_up, w_down):
    depth = w_in.shape[0]
    o1 = Q_LORA_RANK
    o2 = o1 + KV_LORA_RANK
    o3 = o2 + QK_ROPE_DIM
    half = QK_ROPE_DIM // 2
    zeros = jnp.zeros
    w_in, w_uq, w_ukv = w_in.astype(bf16), w_uq.astype(bf16), w_ukv.astype(bf16)

    w_kr = w_in[:, :, o2:o3]
    w_in_pad = jnp.concatenate([w_in[:, :, :o2], w_in[:, :, o3:], zeros((depth, D_MODEL, ROPE_OFF), bf16),
                                w_kr, -w_kr[..., half:], w_kr[..., :half]], axis=-1)

    w_uq_t = jnp.swapaxes(w_uq, 1, 2)

    wkv = w_ukv.reshape(depth, KV_LORA_RANK, MLA_HEADS, QK_NOPE_DIM + V_HEAD_DIM)
    w_k = jnp.concatenate([wkv[..., :QK_NOPE_DIM],
                           zeros((depth, KV_LORA_RANK, MLA_HEADS, HEAD_PAD - QK_NOPE_DIM), bf16)], axis=-1)
    w_k = w_k.reshape(depth, KV_LORA_RANK, MLA_HEADS * HEAD_PAD)
    w_uv_t = jnp.swapaxes(wkv[..., QK_NOPE_DIM:].reshape(depth, KV_LORA_RANK, MLA_WIDTH), 1, 2)

    n_groups = len(POOL_WINDOWS)
    eye = jnp.eye(n_groups, dtype=f32)
    w_pool_bd = (eye[None, :, None, :, None] * w_pool[:, :, :, None, :]).reshape(depth, POOL_WIDTH, POOL_WIDTH)

    sg_bias = jnp.repeat(jnp.swapaxes(b_s, 1, 2), SG_HEAD_DIM, axis=2)
    return dict(
        mix_norm=mix_norm.reshape(depth, 1, D_MODEL),
        w_in=w_in_pad,
        q_norm=q_norm.reshape(depth, 1, Q_LORA_RANK),
        kv_norm=kv_norm.reshape(depth, 1, KV_LORA_RANK),
        w_uq_t=w_uq_t,
        w_k=w_k,
        w_uv_t=w_uv_t,
        w_s=jnp.swapaxes(w_s, 1, 2).reshape(depth, SG_CHUNK, SG_HEADS * SG_CHUNK).astype(bf16),
        sg_bias=sg_bias,
        sg_norm=sg_norm.reshape(depth, 1, SG_WIDTH),
        w_pool=w_pool_bd.astype(bf16),
        pool_scale=pool_scale.reshape(depth, 1, POOL_WIDTH),
        w_o=w_o.astype(bf16),
        ffn_norm=ffn_norm.reshape(depth, 1, D_MODEL),
        w_gate=w_gate.astype(bf16),
        w_up=w_up.astype(bf16),
        w_down=w_down.astype(bf16),
    )


def kernel(x, positions, mix_norm, w_in, q_norm, kv_norm, w_uq, w_ukv, w_pool, pool_scale, sg_norm, w_s, b_s,
           w_o, ffn_norm, w_gate, w_up, w_down, final_norm):
    batch, seq, _ = x.shape
    depth = w_in.shape[0]
    assert seq % TM_PROJ == 0 and seq % TM_POST == 0 and seq % (TQ * Q_SUBTILES) == 0 and seq % TK == 0
    tables = _rope_tables(positions)
    x2 = x.reshape(batch * seq, D_MODEL)
    fin = final_norm.reshape(1, D_MODEL)
    w = _prep_weights(mix_norm, w_in, q_norm, kv_norm, w_uq, w_ukv, w_pool, pool_scale, sg_norm, w_s, b_s,
                      w_o, ffn_norm, w_gate, w_up, w_down)
    for l in range(depth):
        q_t, k, v_t, hp, c = _proj_call(x2, w, l, tables, batch, seq)
        a = _attn_call(q_t, k, v_t)
        x2 = _post_call(x2, a.reshape(batch * seq, MLA_WIDTH), hp, c, w, l, fin, seq, last=(l == depth - 1))
    return x2.reshape(batch, seq, D_MODEL)
```

```python
import functools

import jax
import jax.numpy as jnp
from jax import lax
from jax.experimental import pallas as pl
from jax.experimental.pallas import tpu as pltpu

D_MODEL = 1024
MLA_HEADS = 8
QK_NOPE_DIM = 64
QK_ROPE_DIM = 32
V_HEAD_DIM = 64
Q_LORA_RANK = 384
KV_LORA_RANK = 256
ROPE_THETA = 10000.0
MLA_WIDTH = MLA_HEADS * V_HEAD_DIM
POOL_WINDOWS = (2, 4, 8, 16)
POOL_GROUP_DIM = 64
POOL_WIDTH = len(POOL_WINDOWS) * POOL_GROUP_DIM
SG_HEADS = 4
SG_HEAD_DIM = 64
SG_WIDTH = SG_HEADS * SG_HEAD_DIM
SG_CHUNK = 128
D_FF = 2816
EPS = 1e-6
LOG2_E = 1.4426950408889634

LANES = 128
HEAD_PAD = 128
ROPE_OFF = QK_NOPE_DIM
BF16_ROWS = 16
V_PAD = V_HEAD_DIM + BF16_ROWS
POOL_HALO = 8
FF_CHUNK = 256

C_Q = 0
C_KV = C_Q + Q_LORA_RANK
C_POOL = C_KV + KV_LORA_RANK
C_SG = C_POOL + POOL_WIDTH
C_ROPE = C_SG + 2 * SG_WIDTH
IN_COLS_PAD = C_ROPE + LANES
QK_DIM = QK_NOPE_DIM + QK_ROPE_DIM

TM_PROJ = 512
TM_POST = 512
TQ = 512
TK = 256
Q_SUBTILES = 2
TM_ROPE = 2048

V7X_VMEM_BYTES = 64 * 1024 * 1024
VMEM_LIMIT = V7X_VMEM_BYTES * 7 // 8

f32 = jnp.float32
bf16 = jnp.bfloat16


def _rms(x, g):
    return x * lax.rsqrt(jnp.mean(x * x, axis=-1, keepdims=True) + EPS) * g


def _dot(a, b):
    return jnp.dot(a, b, preferred_element_type=f32)


def _dot_nt(a, b):
    return lax.dot_general(a, b, (((1,), (1,)), ((), ())), preferred_element_type=f32)


def _rope_kernel(pos_ref, inv_ref, cos_t_ref, sin_t_ref, cos_k_ref, sin_k_ref):
    tm = pos_ref.shape[1]
    ang_t = pos_ref[...].astype(f32) * inv_ref[...]
    cos_t = jnp.cos(ang_t)
    sin_t = jnp.sin(ang_t)
    cos_t_ref[...] = cos_t
    sin_t_ref[...] = sin_t
    above = jnp.zeros((ROPE_OFF, tm), f32)
    below = jnp.zeros((HEAD_PAD - ROPE_OFF - QK_ROPE_DIM, tm), f32)
    cos_k_ref[...] = jnp.concatenate([above, cos_t, below], axis=0).T
    sin_k_ref[...] = jnp.concatenate([above, sin_t, below], axis=0).T


def _rope_tables(positions):
    n = positions.size
    inv_freq = ROPE_THETA ** (-jnp.arange(0, QK_ROPE_DIM, 2, dtype=f32) / QK_ROPE_DIM)
    inv2 = jnp.concatenate([inv_freq, inv_freq])
    tm = TM_ROPE
    return pl.pallas_call(
        _rope_kernel,
        grid=(n // tm,),
        in_specs=[
            pl.BlockSpec((1, tm), lambda i: (0, i)),
            pl.BlockSpec((QK_ROPE_DIM, 1), lambda i: (0, 0)),
        ],
        out_specs=[
            pl.BlockSpec((QK_ROPE_DIM, tm), lambda i: (0, i)),
            pl.BlockSpec((QK_ROPE_DIM, tm), lambda i: (0, i)),
            pl.BlockSpec((tm, HEAD_PAD), lambda i: (i, 0)),
            pl.BlockSpec((tm, HEAD_PAD), lambda i: (i, 0)),
        ],
        out_shape=[
            jax.ShapeDtypeStruct((QK_ROPE_DIM, n), f32),
            jax.ShapeDtypeStruct((QK_ROPE_DIM, n), f32),
            jax.ShapeDtypeStruct((n, HEAD_PAD), f32),
            jax.ShapeDtypeStruct((n, HEAD_PAD), f32),
        ],
        compiler_params=pltpu.CompilerParams(dimension_semantics=("parallel",)),
        name="rope_tables",
    )(positions.reshape(1, n), inv2.reshape(QK_ROPE_DIM, 1))


def _proj_kernel(x_ref, g_ref, win_ref, qn_ref, kvn_ref, wuq_t_ref, wk_ref, wuv_t_ref,
                 ws_ref, bias_ref, sgn_ref, cos_t_ref, sin_t_ref, cos_k_ref, sin_k_ref,
                 q_t_ref, k_ref, v_t_ref, hp_ref, c_ref):
    tm = x_ref.shape[0]
    scale = (QK_NOPE_DIM + QK_ROPE_DIM) ** -0.5 * LOG2_E
    h = _rms(x_ref[...], g_ref[...]).astype(bf16)
    p = _dot(h, win_ref[...])

    cq = _rms(p[:, C_Q:C_Q + Q_LORA_RANK], qn_ref[...]).astype(bf16)
    q_t = _dot_nt(wuq_t_ref[...], cq)
    cos_t = cos_t_ref[...]
    sin_t = sin_t_ref[...]
    half = QK_ROPE_DIM // 2
    for hd in range(MLA_HEADS):
        b0 = hd * QK_DIM
        o0 = hd * HEAD_PAD
        nope = q_t[b0:b0 + QK_NOPE_DIM]
        rope = q_t[b0 + QK_NOPE_DIM:b0 + QK_DIM]
        rot = jnp.concatenate([-rope[half:], rope[:half]], axis=0)
        q_t_ref[o0:o0 + QK_NOPE_DIM, :] = (nope * scale).astype(bf16)
        q_t_ref[o0 + ROPE_OFF:o0 + ROPE_OFF + QK_ROPE_DIM, :] = (
            (rope * cos_t + rot * sin_t) * scale).astype(bf16)
        q_t_ref[o0 + ROPE_OFF + QK_ROPE_DIM:o0 + HEAD_PAD, :] = jnp.zeros(
            (HEAD_PAD - ROPE_OFF - QK_ROPE_DIM, tm), bf16)

    ckv = _rms(p[:, C_KV:C_KV + KV_LORA_RANK], kvn_ref[...]).astype(bf16)
    k_nope = _dot(ckv, wk_ref[...])
    rope_blk = p[:, C_ROPE:C_ROPE + LANES]
    rot_blk = pltpu.roll(rope_blk, LANES - QK_ROPE_DIM, axis=1)
    k_rope = rope_blk * cos_k_ref[...] + rot_blk * sin_k_ref[...]
    for hd in range(MLA_HEADS):
        b0 = hd * HEAD_PAD
        k_ref[:, b0:b0 + HEAD_PAD] = (k_nope[:, b0:b0 + HEAD_PAD] + k_rope).astype(bf16)
    v_t = _dot_nt(wuv_t_ref[...], ckv).astype(bf16)
    for hd in range(MLA_HEADS):
        b0 = hd * V_PAD
        v_t_ref[b0:b0 + V_HEAD_DIM, :] = v_t[hd * V_HEAD_DIM:(hd + 1) * V_HEAD_DIM]
        v_t_ref[b0 + V_HEAD_DIM:b0 + V_PAD, :] = jnp.ones((V_PAD - V_HEAD_DIM, tm), bf16)

    hp_ref[...] = p[:, C_POOL:C_POOL + POOL_WIDTH]

    z = jax.nn.gelu(p[:, C_SG:C_SG + 2 * SG_WIDTH])
    u = z[:, :SG_WIDTH]
    v = z[:, SG_WIDTH:]
    grp = lax.broadcasted_iota(jnp.int32, (1, SG_WIDTH), 1) // SG_HEAD_DIM
    vsq = v * v
    inv = jnp.zeros_like(v)
    for g in range(SG_HEADS):
        sel = grp == g
        ms = jnp.sum(jnp.where(sel, vsq, 0.0), axis=-1, keepdims=True) / SG_HEAD_DIM
        inv = jnp.where(sel, lax.rsqrt(ms + EPS), inv)
    vn = (v * inv * sgn_ref[...]).astype(bf16)
    bias = bias_ref[...]
    ws_cat = ws_ref[...]
    zero = jnp.zeros((), bf16)
    for c in range(tm // SG_CHUNK):
        r0 = c * SG_CHUNK
        vc = vn[r0:r0 + SG_CHUNK]
        v_blocks = jnp.concatenate([jnp.where(grp == g, vc, zero) for g in range(SG_HEADS)], axis=0)
        mixed = _dot(ws_cat, v_blocks) + bias
        c_ref[r0:r0 + SG_CHUNK, :] = (u[r0:r0 + SG_CHUNK] * mixed).astype(bf16)


def _const_spec(shape):
    nd = len(shape)
    return pl.BlockSpec(shape, lambda *_: (0,) * nd)


def _layer_spec(shape, l, **kwargs):
    nd = len(shape)
    return pl.BlockSpec((None,) + tuple(shape), lambda *_: (l,) + (0,) * nd, **kwargs)


def _proj_call(x2, w, l, tables, batch, seq):
    n = x2.shape[0]
    tm = TM_PROJ
    cos_t, sin_t, cos_k, sin_k = tables
    row = lambda cols: pl.BlockSpec((tm, cols), lambda i: (i, 0))
    col = lambda rows: pl.BlockSpec((rows, tm), lambda i: (0, i))
    tiles_per_seq = seq // tm
    q_spec = pl.BlockSpec((None, MLA_HEADS * HEAD_PAD, tm), lambda i: (i // tiles_per_seq, 0, i % tiles_per_seq))
    v_spec = pl.BlockSpec((None, MLA_HEADS * V_PAD, tm), lambda i: (i // tiles_per_seq, 0, i % tiles_per_seq))
    k_spec = pl.BlockSpec((None, tm, MLA_HEADS * HEAD_PAD), lambda i: (i // tiles_per_seq, i % tiles_per_seq, 0))
    return pl.pallas_call(
        _proj_kernel,
        grid=(n // tm,),
        in_specs=[
            row(D_MODEL),
            _layer_spec((1, D_MODEL), l),
            _layer_spec((D_MODEL, IN_COLS_PAD), l),
            _layer_spec((1, Q_LORA_RANK), l),
            _layer_spec((1, KV_LORA_RANK), l),
            _layer_spec((MLA_HEADS * QK_DIM, Q_LORA_RANK), l),
            _layer_spec((KV_LORA_RANK, MLA_HEADS * HEAD_PAD), l),
            _layer_spec((MLA_WIDTH, KV_LORA_RANK), l),
            _layer_spec((SG_CHUNK, SG_HEADS * SG_CHUNK), l),
            _layer_spec((SG_CHUNK, SG_WIDTH), l),
            _layer_spec((1, SG_WIDTH), l),
            col(QK_ROPE_DIM),
            col(QK_ROPE_DIM),
            row(HEAD_PAD),
            row(HEAD_PAD),
        ],
        out_specs=[q_spec, k_spec, v_spec, row(POOL_WIDTH), row(SG_WIDTH)],
        out_shape=[
            jax.ShapeDtypeStruct((batch, MLA_HEADS * HEAD_PAD, seq), bf16),
            jax.ShapeDtypeStruct((batch, seq, MLA_HEADS * HEAD_PAD), bf16),
            jax.ShapeDtypeStruct((batch, MLA_HEADS * V_PAD, seq), bf16),
            jax.ShapeDtypeStruct((n, POOL_WIDTH), f32),
            jax.ShapeDtypeStruct((n, SG_WIDTH), bf16),
        ],
        compiler_params=pltpu.CompilerParams(dimension_semantics=("parallel",), vmem_limit_bytes=VMEM_LIMIT),
        name="proj",
    )(x2, w["mix_norm"], w["w_in"], w["q_norm"], w["kv_norm"], w["w_uq_t"], w["w_k"], w["w_uv_t"],
      w["w_s"], w["sg_bias"], w["sg_norm"], cos_t, sin_t, cos_k, sin_k)


def _attn_kernel(q_t_ref, k_ref, v_t_ref, o_ref, s_buf, p_buf, acc_ref, *, tq, tk):
    seq = k_ref.shape[0]
    n_kv = seq // tk
    assert n_kv >= 3
    n_sub = q_t_ref.shape[1] // tq

    def stream(sid, qs, hh):
        q_t = q_t_ref[hh * HEAD_PAD:hh * HEAD_PAD + QK_DIM, qs * tq:(qs + 1) * tq]

        def stage_a(t):
            s_t = _dot(k_ref[t * tk:(t + 1) * tk, hh * HEAD_PAD:hh * HEAD_PAD + QK_DIM], q_t)
            s_buf[sid, t % 2] = s_t
            return jnp.max(s_t, axis=0, keepdims=True)

        def stage_b(t, cmax, m):
            m_new = cmax if m is None else jnp.maximum(m, cmax)
            p_buf[sid, t % 2] = jnp.exp2(s_buf[sid, t % 2] - m_new).astype(bf16)
            alpha = None if m is None else jnp.exp2(m - m_new)
            return m_new, alpha

        def stage_c(t, alpha):
            v_c = v_t_ref[hh * V_PAD:(hh + 1) * V_PAD, t * tk:(t + 1) * tk]
            part = _dot(v_c, p_buf[sid, t % 2])
            acc_ref[sid] = part if alpha is None else alpha * acc_ref[sid] + part

        cmax, m, alphas = None, None, {}
        for t in range(n_kv + 2):
            cmax_new = stage_a(t) if t < n_kv else None
            if 1 <= t <= n_kv:
                m, alphas[t - 1] = stage_b(t - 1, cmax, m)
            if t >= 2:
                stage_c(t - 2, alphas.pop(t - 2))
            cmax = cmax_new
        acc = acc_ref[sid]
        return acc[:V_HEAD_DIM] / acc[V_HEAD_DIM:V_HEAD_DIM + 1]

    for qs in range(n_sub):
        o_t = jnp.concatenate([stream(2 * qs + hh, qs, hh) for hh in range(2)], axis=0)
        o_ref[qs * tq:(qs + 1) * tq, :] = o_t.T.astype(o_ref.dtype)


def _attn_call(q_t, k, v_t):
    batch, _, seq = q_t.shape
    pairs = MLA_HEADS // 2
    n_streams = 2 * Q_SUBTILES
    tq_blk = TQ * Q_SUBTILES
    return pl.pallas_call(
        functools.partial(_attn_kernel, tq=TQ, tk=TK),
        grid=(batch, pairs, seq // tq_blk),
        in_specs=[
            pl.BlockSpec((None, 2 * HEAD_PAD, tq_blk), lambda b, h, q: (b, h, q)),
            pl.BlockSpec((None, seq, 2 * HEAD_PAD), lambda b, h, q: (b, 0, h)),
            pl.BlockSpec((None, 2 * V_PAD, seq), lambda b, h, q: (b, h, 0)),
        ],
        out_specs=pl.BlockSpec((None, tq_blk, 2 * V_HEAD_DIM), lambda b, h, q: (b, q, h)),
        out_shape=jax.ShapeDtypeStruct((batch, seq, MLA_WIDTH), bf16),
        scratch_shapes=[pltpu.VMEM((n_streams, 2, TK, TQ), f32), pltpu.VMEM((n_streams, 2, TK, TQ), bf16),
                        pltpu.VMEM((n_streams, V_PAD, TQ), f32)],
        compiler_params=pltpu.CompilerParams(
            dimension_semantics=("parallel", "parallel", "arbitrary"), vmem_limit_bytes=VMEM_LIMIT),
        name="attn",
    )(q_t, k, v_t)


def _post_mix(x_ref, a_ref, hp_ref, hp_prev_ref, hp_next_ref, c_ref, wo_ref, wpool_ref, pscale_ref, fn_ref,
              ext_ref, a1_ref, a2_ref, a4_ref, *, si, seq):
    tm = x_ref.shape[0]
    tiles_per_seq = seq // tm
    hp = hp_ref[...]
    assert POOL_WINDOWS == (2, 4, 8, 16) and POOL_WIDTH == 2 * LANES

    h8 = POOL_HALO
    t0 = 2 * h8
    rows = tm + 3 * h8
    zero_rows = jnp.zeros((h8, POOL_WIDTH), f32)
    ext_ref[0:h8, :] = zero_rows
    ext_ref[h8:t0, :] = jnp.where(si > 0, hp_prev_ref[...], 0.0)
    ext_ref[t0:t0 + tm, :] = hp
    ext_ref[t0 + tm:rows, :] = jnp.where(si < tiles_per_seq - 1, hp_next_ref[...], 0.0)
    a1_ref[0:h8, :] = zero_rows
    a2_ref[0:h8, :] = zero_rows[:, :LANES]
    a1_ref[h8:rows, :] = ext_ref[h8:rows, :] + ext_ref[h8 - 1:rows - 1, :]
    a2_ref[h8:rows, :] = a1_ref[h8:rows, LANES:] + a1_ref[h8 - 2:rows - 2, LANES:]
    a4_ref[h8:rows, :] = a2_ref[h8:rows, :] + a2_ref[h8 - 4:rows - 4, :]

    def window(ref, lanes, half_w):
        end = t0 + half_w - 1
        return ref[end:end + tm, lanes] + ref[end - half_w:end - half_w + tm, lanes]

    lo = slice(0, LANES)
    sums = [a1_ref[t0:t0 + tm, lo], window(a1_ref, lo, 2), window(a2_ref, slice(None), 4),
            window(a4_ref, slice(None), 8)]
    t_abs = si * tm + lax.broadcasted_iota(jnp.int32, (tm, 1), 0)
    counts = []
    for w in POOL_WINDOWS:
        left = w // 2
        right = w - 1 - left
        counts.append((jnp.minimum(t_abs + right + 1, seq) - jnp.maximum(t_abs - left, 0)).astype(f32))
    first = lax.broadcasted_iota(jnp.int32, (1, LANES), 1) < POOL_GROUP_DIM
    mean = jnp.concatenate(
        [jnp.where(first, sums[0], sums[1]) / jnp.where(first, counts[0], counts[1]),
         jnp.where(first, sums[2], sums[3]) / jnp.where(first, counts[2], counts[3])], axis=1)
    d = (mean - hp).astype(bf16)
    b = _dot(d, wpool_ref[...]) * pscale_ref[...]

    mix = jnp.concatenate([a_ref[...], b.astype(bf16), c_ref[...]], axis=1)
    x1 = x_ref[...] + _dot(mix, wo_ref[...])
    return x1, _rms(x1, fn_ref[...]).astype(bf16)


def _post_ffn(x1, hn, wgu_ref, wd_ref, final_ref, *, last):
    y = None
    for f in range(D_FF // FF_CHUNK):
        f0 = f * FF_CHUNK
        gu = _dot(hn, wgu_ref[:, 2 * f0:2 * f0 + 2 * FF_CHUNK])
        gte = gu[:, :FF_CHUNK]
        up = gu[:, FF_CHUNK:]
        act = (jax.nn.silu(gte) * up).astype(bf16)
        part = _dot(act, wd_ref[f0:f0 + FF_CHUNK, :])
        y = part if y is None else y + part
    out = x1 + y
    return _rms(out, final_ref[...]) if last else out


def _post_kernel(x_ref, a_ref, hp_ref, hp_prev_ref, hp_next_ref, c_ref, wo_ref, wpool_ref, pscale_ref,
                 fn_ref, wgu_ref, wd_ref, final_ref, o_ref, ext_ref, a1_ref, a2_ref, a4_ref, *, seq, last):
    tm = x_ref.shape[0]
    si = pl.program_id(0) % (seq // tm)
    x1, hn = _post_mix(x_ref, a_ref, hp_ref, hp_prev_ref, hp_next_ref, c_ref, wo_ref, wpool_ref, pscale_ref,
                       fn_ref, ext_ref, a1_ref, a2_ref, a4_ref, si=si, seq=seq)
    o_ref[...] = _post_ffn(x1, hn, wgu_ref, wd_ref, final_ref, last=last)


def _post_call(x2, a2, hp, c, w, l, final_norm, seq, last):
    n = x2.shape[0]
    tm = TM_POST
    hb = tm // POOL_HALO
    n_halo_blocks = n // POOL_HALO
    ext_rows = tm + 3 * POOL_HALO
    row = lambda cols: pl.BlockSpec((tm, cols), lambda i: (i, 0))
    resident = lambda shape: _layer_spec(shape, l, pipeline_mode=pl.Buffered(1))
    return pl.pallas_call(
        functools.partial(_post_kernel, seq=seq, last=last),
        grid=(n // tm,),
        in_specs=[
            row(D_MODEL),
            row(MLA_WIDTH),
            row(POOL_WIDTH),
            pl.BlockSpec((POOL_HALO, POOL_WIDTH), lambda i: (jnp.maximum(i * hb - 1, 0), 0)),
            pl.BlockSpec((POOL_HALO, POOL_WIDTH), lambda i: (jnp.minimum((i + 1) * hb, n_halo_blocks - 1), 0)),
            row(SG_WIDTH),
            resident((D_MODEL, D_MODEL)),
            resident((POOL_WIDTH, POOL_WIDTH)),
            _layer_spec((1, POOL_WIDTH), l),
            _layer_spec((1, D_MODEL), l),
            resident((D_MODEL, 2 * D_FF)),
            resident((D_FF, D_MODEL)),
            _const_spec((1, D_MODEL)),
        ],
        out_specs=row(D_MODEL),
        out_shape=jax.ShapeDtypeStruct((n, D_MODEL), f32),
        scratch_shapes=[pltpu.VMEM((ext_rows, POOL_WIDTH), f32), pltpu.VMEM((ext_rows, POOL_WIDTH), f32),
                        pltpu.VMEM((ext_rows, LANES), f32), pltpu.VMEM((ext_rows, LANES), f32)],
        compiler_params=pltpu.CompilerParams(dimension_semantics=("parallel",), vmem_limit_bytes=VMEM_LIMIT),
        name="post_last" if last else "post",
    )(x2, a2, hp, hp, hp, c, w["w_o"], w["w_pool"], w["pool_scale"], w["ffn_norm"],
      w["w_gate_up"], w["w_down"], final_norm)


def _prep_weights(mix_norm, w_in, q_norm, kv_norm, w_uq, w_ukv, w_pool, pool_scale, sg_norm, w_s, b_s,
                  w_o, ffn_norm, w_gate, w_up, w_down):
    depth = w_in.shape[0]
    o1 = Q_LORA_RANK
    o2 = o1 + KV_LORA_RANK
    o3 = o2 + QK_ROPE_DIM
    half = QK_ROPE_DIM // 2
    zeros = jnp.zeros
    w_in, w_uq, w_ukv = w_in.astype(bf16), w_uq.astype(bf16), w_ukv.astype(bf16)

    w_kr = w_in[:, :, o2:o3]
    w_in_pad = jnp.concatenate([w_in[:, :, :o2], w_in[:, :, o3:], zeros((depth, D_MODEL, ROPE_OFF), bf16),
                                w_kr, -w_kr[..., half:], w_kr[..., :half]], axis=-1)

    w_uq_t = jnp.swapaxes(w_uq, 1, 2)

    wkv = w_ukv.reshape(depth, KV_LORA_RANK, MLA_HEADS, QK_NOPE_DIM + V_HEAD_DIM)
    w_k = jnp.concatenate([wkv[..., :QK_NOPE_DIM],
                           zeros((depth, KV_LORA_RANK, MLA_HEADS, HEAD_PAD - QK_NOPE_DIM), bf16)], axis=-1)
    w_k = w_k.reshape(depth, KV_LORA_RANK, MLA_HEADS * HEAD_PAD)
    w_uv_t = jnp.swapaxes(wkv[..., QK_NOPE_DIM:].reshape(depth, KV_LORA_RANK, MLA_WIDTH), 1, 2)

    n_groups = len(POOL_WINDOWS)
    eye = jnp.eye(n_groups, dtype=f32)
    w_pool_bd = (eye[None, :, None, :, None] * w_pool[:, :, :, None, :]).reshape(depth, POOL_WIDTH, POOL_WIDTH)

    sg_bias = jnp.repeat(jnp.swapaxes(b_s, 1, 2), SG_HEAD_DIM, axis=2)
    return dict(
        mix_norm=mix_norm.reshape(depth, 1, D_MODEL),
        w_in=w_in_pad,
        q_norm=q_norm.reshape(depth, 1, Q_LORA_RANK),
        kv_norm=kv_norm.reshape(depth, 1, KV_LORA_RANK),
        w_uq_t=w_uq_t,
        w_k=w_k,
        w_uv_t=w_uv_t,
        w_s=jnp.swapaxes(w_s, 1, 2).reshape(depth, SG_CHUNK, SG_HEADS * SG_CHUNK).astype(bf16),
        sg_bias=sg_bias,
        sg_norm=sg_norm.reshape(depth, 1, SG_WIDTH),
        w_pool=w_pool_bd.astype(bf16),
        pool_scale=pool_scale.reshape(depth, 1, POOL_WIDTH),
        w_o=w_o.astype(bf16),
        ffn_norm=ffn_norm.reshape(depth, 1, D_MODEL),
        w_gate_up=jnp.stack([w_gate.astype(bf16).reshape(depth, D_MODEL, D_FF // FF_CHUNK, FF_CHUNK),
                             w_up.astype(bf16).reshape(depth, D_MODEL, D_FF // FF_CHUNK, FF_CHUNK)],
                            axis=3).reshape(depth, D_MODEL, 2 * D_FF),
        w_down=w_down.astype(bf16),
    )


def kernel(x, positions, mix_norm, w_in, q_norm, kv_norm, w_uq, w_ukv, w_pool, pool_scale, sg_norm, w_s, b_s,
           w_o, ffn_norm, w_gate, w_up, w_down, final_norm):
    batch, seq, _ = x.shape
    depth = w_in.shape[0]
    assert seq % TM_PROJ == 0 and seq % TM_POST == 0 and seq % (TQ * Q_SUBTILES) == 0 and seq % TK == 0
    tables = _rope_tables(positions)
    x2 = x.reshape(batch * seq, D_MODEL)
    fin = final_norm.reshape(1, D_MODEL)
    w = _prep_weights(mix_norm, w_in, q_norm, kv_norm, w_uq, w_ukv, w_pool, pool_scale, sg_norm, w_s, b_s,
                      w_o, ffn_norm, w_gate, w_up, w_down)
    for l in range(depth):
        q_t, k, v_t, hp, c = _proj_call(x2, w, l, tables, batch, seq)
        a = _attn_call(q_t, k, v_t)
        x2 = _post_call(x2, a.reshape(batch * seq, MLA_WIDTH), hp, c, w, l, fin, seq, last=(l == depth - 1))
    return x2.reshape(batch, seq, D_MODEL)
```
